```python
import math
import jax, jax.numpy as jnp
from jax import lax
import numpy as np

D_MODEL = 1024
BATCH = 16
SEQ = 2048
DEPTH = 1
DEC_BATCH = 2
DEC_SEQ = 8192
PAST_LEN = 128

MIX_WIDTH = D_MODEL
MLA_WIDTH = MIX_WIDTH // 2
NA_WIDTH = MIX_WIDTH - MLA_WIDTH

MLA_HEADS = 8
MLA_V_DIM = MLA_WIDTH // MLA_HEADS
MLA_NOPE_DIM = 64
MLA_ROPE_DIM = 32
MLA_QK_DIM = MLA_NOPE_DIM + MLA_ROPE_DIM
Q_LORA_RANK = 384
KV_LORA_RANK = 256
ROPE_THETA = 10000.0
Q_BLOCK = 128

NA_HEADS = 8
NA_HEAD_DIM = NA_WIDTH // NA_HEADS
GRID_W = 64
NA_KH = 8
NA_KW = 16

IN_COLS = Q_LORA_RANK + KV_LORA_RANK + MLA_ROPE_DIM + 3 * NA_WIDTH
D_FF = int(math.ceil(8 * D_MODEL / 3 / 256)) * 256
EPS = 1e-6
NEG_INF = -1e30

kernel_name = "hymba_mla_natten_encoder"


def rmsnorm(x, g):
    xf = x.astype(jnp.float32)
    y = xf * lax.rsqrt(jnp.mean(xf * xf, axis=-1, keepdims=True) + EPS)
    return (y * g.astype(jnp.float32)).astype(x.dtype)


def rope_tables(seq_len):
    inv = ROPE_THETA ** (-jnp.arange(0, MLA_ROPE_DIM, 2, dtype=jnp.float32) / MLA_ROPE_DIM)
    ang = jnp.arange(seq_len, dtype=jnp.float32)[:, None] * inv[None, :]
    return jnp.cos(ang), jnp.sin(ang)


def apply_rope(x, cos, sin):
    shape = (cos.shape[0],) + (1,) * (x.ndim - 3) + (cos.shape[1],)
    c, s = cos.reshape(shape), sin.reshape(shape)
    xf = x.astype(jnp.float32)
    x1, x2 = jnp.split(xf, 2, axis=-1)
    return jnp.concatenate([x1 * c - x2 * s, x1 * s + x2 * c], axis=-1).astype(x.dtype)


def mla_group(c_q, c_kv, k_r, q_norm_g, kv_norm_g, w_uq, w_ukv):
    B, S, _ = c_q.shape
    q = (rmsnorm(c_q, q_norm_g) @ w_uq).reshape(B, S, MLA_HEADS, MLA_QK_DIM)
    kv = (rmsnorm(c_kv, kv_norm_g) @ w_ukv).reshape(B, S, MLA_HEADS, MLA_NOPE_DIM + MLA_V_DIM)
    k_nope, v = kv[..., :MLA_NOPE_DIM], kv[..., MLA_NOPE_DIM:]
    cos, sin = rope_tables(S)
    q = jnp.concatenate([q[..., :MLA_NOPE_DIM], apply_rope(q[..., MLA_NOPE_DIM:], cos, sin)], axis=-1)
    q = q * (MLA_QK_DIM ** -0.5)
    k_rope = apply_rope(k_r, cos, sin)
    n_blk = S // Q_BLOCK
    qb = q.reshape(B, n_blk, Q_BLOCK, MLA_HEADS, MLA_QK_DIM).transpose(1, 0, 2, 3, 4)

    def block(qi):
        s = (jnp.einsum('bqhd,bkhd->bhqk', qi[..., :MLA_NOPE_DIM], k_nope,
                        preferred_element_type=jnp.float32)
             + jnp.einsum('bqhr,bkr->bhqk', qi[..., MLA_NOPE_DIM:], k_rope,
                          preferred_element_type=jnp.float32))
        p = jax.nn.softmax(s, axis=-1).astype(v.dtype)
        return jnp.einsum('bhqk,bkhd->bqhd', p, v)

    o = lax.map(block, qb)
    return o.transpose(1, 0, 2, 3, 4).reshape(B, S, MLA_WIDTH)


def na_group(qkv, rpb):
    B, S, _ = qkv.shape
    rows = S // GRID_W
    kh = min(NA_KH, rows)
    qkv = qkv.reshape(B, rows, GRID_W, 3, NA_HEADS, NA_HEAD_DIM)
    q = qkv[:, :, :, 0] * (NA_HEAD_DIM ** -0.5)
    k = qkv[:, :, :, 1]
    v = qkv[:, :, :, 2]
    qc = np.arange(GRID_W)[:, None]
    kc = np.arange(GRID_W)[None, :]
    col_start = np.clip(qc - NA_KW // 2, 0, GRID_W - NA_KW)
    col_mask = (kc >= col_start) & (kc < col_start + NA_KW)
    dj_idx = (np.clip(kc - qc, -(NA_KW - 1), NA_KW - 1) + NA_KW - 1).astype(np.int32)

    def row_block(r):
        start = jnp.clip(r - kh // 2, 0, rows - kh)
        q_r = lax.dynamic_index_in_dim(q, r, axis=1, keepdims=False)
        k_b = lax.dynamic_slice_in_dim(k, start, kh, axis=1)
        v_b = lax.dynamic_slice_in_dim(v, start, kh, axis=1)
        s = jnp.einsum('bqhd,bkwhd->bhqkw', q_r, k_b, preferred_element_type=jnp.float32)
        di = start + jnp.arange(kh) - r
        bias = rpb[:, di + NA_KH - 1][:, :, dj_idx]
        bias = bias.transpose(0, 2, 1, 3).astype(jnp.float32)
        s = jnp.where(col_mask[None, None, :, None, :], s + bias[None], NEG_INF)
        p = jax.nn.softmax(s.reshape(B, NA_HEADS, GRID_W, kh * GRID_W), axis=-1)
        p = p.reshape(B, NA_HEADS, GRID_W, kh, GRID_W).astype(v.dtype)
        return jnp.einsum('bhqkw,bkwhd->bqhd', p, v_b)

    o = lax.map(row_block, jnp.arange(rows))
    return o.transpose(1, 0, 2, 3, 4).reshape(B, S, NA_WIDTH)


def encoder_layer(x, attn_norm_g, w_in, q_norm_g, kv_norm_g, w_uq, w_ukv, na_rpb,
                  mla_out_g, na_out_g, w_o, ffn_norm_g, w_gate, w_up, w_down):
    h = rmsnorm(x, attn_norm_g)
    proj = h @ w_in
    o1 = Q_LORA_RANK
    o2 = o1 + KV_LORA_RANK
    o3 = o2 + MLA_ROPE_DIM
    a = mla_group(proj[..., :o1], proj[..., o1:o2], proj[..., o2:o3],
                  q_norm_g, kv_norm_g, w_uq, w_ukv)
    b = na_group(proj[..., o3:], na_rpb)
    mix = jnp.concatenate([rmsnorm(a, mla_out_g), rmsnorm(b, na_out_g)], axis=-1)
    x = x + mix @ w_o
    h2 = rmsnorm(x, ffn_norm_g)
    x = x + (jax.nn.silu(h2 @ w_gate) * (h2 @ w_up)) @ w_down
    return x


def run_trunk(x, attn_norm_g, w_in, q_norm_g, kv_norm_g, w_uq, w_ukv, na_rpb,
              mla_out_g, na_out_g, w_o, ffn_norm_g, w_gate, w_up, w_down, final_norm_g):
    for l in range(DEPTH):
        x = encoder_layer(x, attn_norm_g[l], w_in[l], q_norm_g[l], kv_norm_g[l], w_uq[l],
                          w_ukv[l], na_rpb[l], mla_out_g[l], na_out_g[l], w_o[l],
                          ffn_norm_g[l], w_gate[l], w_up[l], w_down[l])
    return rmsnorm(x, final_norm_g)


def setup_inputs(seed: int = 0) -> dict:
    key = jax.random.key(seed)
    ks = jax.random.split(key, 20)
    f32 = jnp.float32

    def w(k, shape, fan_in):
        return jax.random.normal(k, shape, f32) * (fan_in ** -0.5)

    def gain(k, shape):
        return 1.0 + 0.05 * jax.random.normal(k, shape, f32)

    L = DEPTH
    return {
        "x_prompt": jax.random.normal(ks[0], (BATCH, SEQ, D_MODEL), f32),
        "x_sample": jax.random.normal(ks[1], (DEC_BATCH, DEC_SEQ, D_MODEL), f32),
        "attn_norm_g": gain(ks[2], (L, D_MODEL)),
        "w_in": w(ks[3], (L, D_MODEL, IN_COLS), D_MODEL),
        "q_norm_g": gain(ks[4], (L, Q_LORA_RANK)),
        "kv_norm_g": gain(ks[5], (L, KV_LORA_RANK)),
        "w_uq": w(ks[6], (L, Q_LORA_RANK, MLA_HEADS * MLA_QK_DIM), Q_LORA_RANK),
        "w_ukv": w(ks[7], (L, KV_LORA_RANK, MLA_HEADS * (MLA_NOPE_DIM + MLA_V_DIM)), KV_LORA_RANK),
        "na_rpb": 0.1 * jax.random.normal(ks[8], (L, NA_HEADS, 2 * NA_KH - 1, 2 * NA_KW - 1), f32),
        "mla_out_g": gain(ks[9], (L, MLA_WIDTH)),
        "na_out_g": gain(ks[10], (L, NA_WIDTH)),
        "w_o": w(ks[11], (L, MIX_WIDTH, D_MODEL), MIX_WIDTH),
        "ffn_norm_g": gain(ks[12], (L, D_MODEL)),
        "w_gate": w(ks[13], (L, D_MODEL, D_FF), D_MODEL),
        "w_up": w(ks[14], (L, D_MODEL, D_FF), D_MODEL),
        "w_down": w(ks[15], (L, D_FF, D_MODEL), D_FF),
        "final_norm_g": gain(ks[16], (D_MODEL,)),
    }


def reference(x_prompt, x_sample, attn_norm_g, w_in, q_norm_g, kv_norm_g, w_uq, w_ukv,
              na_rpb, mla_out_g, na_out_g, w_o, ffn_norm_g, w_gate, w_up, w_down,
              final_norm_g):
    y_prompt = run_trunk(x_prompt, attn_norm_g, w_in, q_norm_g, kv_norm_g, w_uq, w_ukv,
                         na_rpb, mla_out_g, na_out_g, w_o, ffn_norm_g, w_gate, w_up,
                         w_down, final_norm_g)
    y_sample = run_trunk(x_sample, attn_norm_g, w_in, q_norm_g, kv_norm_g, w_uq, w_ukv,
                         na_rpb, mla_out_g, na_out_g, w_o, ffn_norm_g, w_gate, w_up,
                         w_down, final_norm_g)
    return (y_prompt, y_sample)
```

```python
import functools
import math

import jax
import jax.numpy as jnp
import numpy as np
from jax import lax
from jax.experimental import pallas as pl
from jax.experimental.pallas import tpu as pltpu

D_MODEL = 1024
MLA_WIDTH = D_MODEL // 2
NA_WIDTH = D_MODEL - MLA_WIDTH
HEADS = 8
HEAD_DIM = 64
MLA_ROPE_DIM = 32
MLA_QK_DIM = HEAD_DIM + MLA_ROPE_DIM
Q_LORA_RANK = 384
KV_LORA_RANK = 256
ROPE_THETA = 10000.0
GRID_W = 64
NA_KH = 8
NA_KW = 16
D_FF = int(math.ceil(8 * D_MODEL / 3 / 256)) * 256
EPS = 1e-6
NEG_INF = -1e30

V7X_LANES = 128
V7X_VMEM_BYTES = 64 * 1024 * 1024

TOKEN_TILE = 512
MLA_Q_TILE = 256
MLA_Q_BLOCK = 2048
NA_ROWS = 4
NA_TOK = NA_ROWS * GRID_W
NA_CHUNKS = 3
FF_CHUNK = 256

_C_Q = 0
_C_KV = _C_Q + Q_LORA_RANK
_C_KR = _C_KV + KV_LORA_RANK
_C_NAQ = _C_KR + V7X_LANES
_C_NAK = _C_NAQ + HEADS * V7X_LANES
_C_END = _C_NAK + NA_WIDTH

_NT = (((1,), (1,)), ((), ()))
_TN = (((0,), (0,)), ((), ()))
BF16 = jnp.bfloat16
F32 = jnp.float32


def _rms_rows(x, g):
    return x * lax.rsqrt(jnp.mean(x * x, axis=-1, keepdims=True) + EPS) * g


def _rms_cols(xt, g):
    return xt * lax.rsqrt(jnp.mean(xt * xt, axis=0, keepdims=True) + EPS) * g


def _rope_lanes(x, cos, sin_lo, sin_hi):
    half = MLA_ROPE_DIM // 2
    return (x * cos + pltpu.roll(x, V7X_LANES - half, 1) * sin_lo
            + pltpu.roll(x, half, 1) * sin_hi)


def _proj_kernel(x_ref, ga_ref, wmain_ref, gq_ref, gkv_ref, wuq_ref, wk_ref, wvt_ref, wnavt_ref,
                 cos_ref, sinlo_ref, sinhi_ref,
                 q_ref, k_ref, vt_ref, naq_ref, nak_ref, navt_ref):
    h = _rms_rows(x_ref[...], ga_ref[...]).astype(BF16)

    def proj(lo, hi):
        return jnp.dot(h, wmain_ref[:, lo:hi], preferred_element_type=F32)

    cqn = _rms_rows(proj(_C_Q, _C_KV), gq_ref[...]).astype(BF16)
    ckvn = _rms_rows(proj(_C_KV, _C_KR), gkv_ref[...]).astype(BF16)
    cos, sin_lo, sin_hi = cos_ref[...], sinlo_ref[...], sinhi_ref[...]
    k_rope = _rope_lanes(proj(_C_KR, _C_NAQ), cos, sin_lo, sin_hi)

    q = jnp.dot(cqn, wuq_ref[...], preferred_element_type=F32)
    kn = jnp.dot(ckvn, wk_ref[...], preferred_element_type=F32)
    vt = lax.dot_general(wvt_ref[...], ckvn, _NT, preferred_element_type=F32)
    naq = proj(_C_NAQ, _C_NAK) * (HEAD_DIM ** -0.5)
    scale = MLA_QK_DIM ** -0.5
    for hd in range(HEADS):
        lanes = slice(hd * V7X_LANES, (hd + 1) * V7X_LANES)
        q_ref[hd] = (_rope_lanes(q[:, lanes], cos, sin_lo, sin_hi) * scale).astype(BF16)
        k_ref[hd] = (kn[:, lanes] + k_rope).astype(BF16)
        vt_ref[hd, 0] = vt[hd * HEAD_DIM:(hd + 1) * HEAD_DIM, :].astype(BF16)
        naq_ref[hd] = naq[:, lanes].astype(BF16)
    nak_ref[...] = proj(_C_NAK, _C_END).astype(BF16)
    navt_ref[...] = lax.dot_general(wnavt_ref[...], h, _NT, preferred_element_type=F32).astype(BF16)


def _proj_call(x2d, seq, w):
    n = x2d.shape[0]
    tm = TOKEN_TILE
    n_tiles = n // tm
    pos_tiles = seq // tm
    const = lambda i: (0, 0)
    single = dict(pipeline_mode=pl.Buffered(1))
    in_specs = [
        pl.BlockSpec((tm, D_MODEL), lambda i: (i, 0)),
        pl.BlockSpec((1, D_MODEL), const),
        pl.BlockSpec((D_MODEL, _C_END), const, **single),
        pl.BlockSpec((1, Q_LORA_RANK), const),
        pl.BlockSpec((1, KV_LORA_RANK), const),
        pl.BlockSpec((Q_LORA_RANK, HEADS * V7X_LANES), const, **single),
        pl.BlockSpec((KV_LORA_RANK, HEADS * V7X_LANES), const, **single),
        pl.BlockSpec((MLA_WIDTH, KV_LORA_RANK), const, **single),
        pl.BlockSpec((NA_WIDTH, D_MODEL), const, **single),
        pl.BlockSpec((tm, V7X_LANES), lambda i: (i % pos_tiles, 0)),
        pl.BlockSpec((tm, V7X_LANES), lambda i: (i % pos_tiles, 0)),
        pl.BlockSpec((tm, V7X_LANES), lambda i: (i % pos_tiles, 0)),
    ]
    out_shape = (
        jax.ShapeDtypeStruct((HEADS, n, V7X_LANES), BF16),
        jax.ShapeDtypeStruct((HEADS, n, V7X_LANES), BF16),
        jax.ShapeDtypeStruct((HEADS, n_tiles, HEAD_DIM, tm), BF16),
        jax.ShapeDtypeStruct((HEADS, n, V7X_LANES), BF16),
        jax.ShapeDtypeStruct((n, NA_WIDTH), BF16),
        jax.ShapeDtypeStruct((NA_WIDTH, n), BF16),
    )
    out_specs = (
        pl.BlockSpec((HEADS, tm, V7X_LANES), lambda i: (0, i, 0)),
        pl.BlockSpec((HEADS, tm, V7X_LANES), lambda i: (0, i, 0)),
        pl.BlockSpec((HEADS, 1, HEAD_DIM, tm), lambda i: (0, i, 0, 0)),
        pl.BlockSpec((HEADS, tm, V7X_LANES), lambda i: (0, i, 0)),
        pl.BlockSpec((tm, NA_WIDTH), lambda i: (i, 0)),
        pl.BlockSpec((NA_WIDTH, tm), lambda i: (0, i)),
    )
    return pl.pallas_call(
        _proj_kernel,
        grid=(n_tiles,),
        in_specs=in_specs,
        out_specs=out_specs,
        out_shape=out_shape,
        compiler_params=pltpu.CompilerParams(
            dimension_semantics=("arbitrary",), vmem_limit_bytes=48 * 1024 * 1024),
        name="proj",
    )(x2d, w["attn_g"], w["w_main"], w["q_g"], w["kv_g"], w["w_uq"], w["w_k"], w["w_vt"],
      w["w_navt"], w["cos"], w["sin_lo"], w["sin_hi"])


def _mla_kernel(q_ref, k_ref, vt_ref, o_ref, *, n_q, n_kv):
    tq, tk = MLA_Q_TILE, TOKEN_TILE

    def q_body(qi, _):
        q0 = pl.multiple_of(qi * tq, tq)
        q = q_ref[0, pl.ds(q0, tq), :]

        def kv_body(j, carry):
            m, l, acc = carry
            k0 = pl.multiple_of(j * tk, tk)
            s = lax.dot_general(k_ref[0, pl.ds(k0, tk), :], q, _NT, preferred_element_type=F32)
            m_new = jnp.maximum(m, jnp.max(s, axis=0, keepdims=True))
            alpha = jnp.exp(m - m_new)
            p = jnp.exp(s - m_new)
            l = alpha * l + jnp.sum(p, axis=0, keepdims=True)
            pv = jnp.dot(vt_ref[0, j], p.astype(BF16), preferred_element_type=F32)
            return m_new, l, alpha * acc + pv

        init = (jnp.full((1, tq), NEG_INF, F32), jnp.zeros((1, tq), F32),
                jnp.zeros((HEAD_DIM, tq), F32))
        _, l, acc = lax.fori_loop(0, n_kv, kv_body, init)
        o_ref[:, pl.ds(q0, tq)] = acc / l
        return 0

    lax.fori_loop(0, n_q, q_body, 0)


def _mla_call(q, k, vt, batch, seq):
    n = q.shape[1]
    qb = min(seq, MLA_Q_BLOCK)
    n_qb = seq // qb
    n_kv = seq // TOKEN_TILE
    kernel = functools.partial(_mla_kernel, n_q=qb // MLA_Q_TILE, n_kv=n_kv)
    return pl.pallas_call(
        kernel,
        grid=(batch, HEADS, n_qb),
        in_specs=[
            pl.BlockSpec((1, qb, V7X_LANES), lambda b, h, i: (h, b * n_qb + i, 0)),
            pl.BlockSpec((1, seq, V7X_LANES), lambda b, h, i: (h, b, 0)),
            pl.BlockSpec((1, n_kv, HEAD_DIM, TOKEN_TILE), lambda b, h, i: (h, b, 0, 0)),
        ],
        out_specs=pl.BlockSpec((HEAD_DIM, qb), lambda b, h, i: (h, b * n_qb + i)),
        out_shape=jax.ShapeDtypeStruct((MLA_WIDTH, n), F32),
        compiler_params=pltpu.CompilerParams(
            dimension_semantics=("arbitrary", "arbitrary", "arbitrary"),
            vmem_limit_bytes=40 * 1024 * 1024),
        name="mla",
    )(q, k, vt)


def _na_kernel(q_ref, k0_ref, k1_ref, k2_ref, v0_ref, v1_ref, v2_ref, bm_ref, o_ref):
    k_refs = (k0_ref, k1_ref, k2_ref)
    v_refs = (v0_ref, v1_ref, v2_ref)
    for hd in range(HEADS):
        q = q_ref[hd]
        pair = slice((hd // 2) * V7X_LANES, (hd // 2 + 1) * V7X_LANES)
        rows = slice(hd * HEAD_DIM, (hd + 1) * HEAD_DIM)
        s = [lax.dot_general(k_refs[t][:, pair], q, _NT, preferred_element_type=F32)
             + bm_ref[0, hd, t] for t in range(NA_CHUNKS)]
        m = functools.reduce(jnp.maximum, [jnp.max(st, axis=0, keepdims=True) for st in s])
        p = [jnp.exp(st - m) for st in s]
        l = functools.reduce(jnp.add, [jnp.sum(pt, axis=0, keepdims=True) for pt in p])
        o = functools.reduce(jnp.add, [
            jnp.dot(v_refs[t][rows, :], p[t].astype(BF16), preferred_element_type=F32)
            for t in range(NA_CHUNKS)])
        o_ref[rows, :] = o / l


def _na_call(naq, nak, navt, bias_mask, batch, seq):
    n = nak.shape[0]
    groups = seq // NA_TOK

    def win(b, g, t):
        return b * groups + jnp.clip(g - 1, 0, groups - NA_CHUNKS) + t

    def kind(g):
        return jnp.where(g == 0, 0, jnp.where(g == groups - 1, 2, 1))

    k_specs = [pl.BlockSpec((NA_TOK, NA_WIDTH), functools.partial(lambda b, g, t: (win(b, g, t), 0), t=t))
               for t in range(NA_CHUNKS)]
    v_specs = [pl.BlockSpec((NA_WIDTH, NA_TOK), functools.partial(lambda b, g, t: (0, win(b, g, t)), t=t))
               for t in range(NA_CHUNKS)]
    return pl.pallas_call(
        _na_kernel,
        grid=(batch, groups),
        in_specs=[pl.BlockSpec((HEADS, NA_TOK, V7X_LANES), lambda b, g: (0, b * groups + g, 0))]
        + k_specs + v_specs
        + [pl.BlockSpec((1, HEADS, NA_CHUNKS, NA_TOK, NA_TOK), lambda b, g: (kind(g), 0, 0, 0, 0))],
        out_specs=pl.BlockSpec((NA_WIDTH, NA_TOK), lambda b, g: (0, b * groups + g)),
        out_shape=jax.ShapeDtypeStruct((NA_WIDTH, n), F32),
        compiler_params=pltpu.CompilerParams(
            dimension_semantics=("arbitrary", "arbitrary"), vmem_limit_bytes=40 * 1024 * 1024),
        name="na",
    )(naq, nak, nak, nak, navt, navt, navt, bias_mask)


def _post_kernel(at_ref, bt_ref, x_ref, ga_ref, gb_ref, wo_ref, gf_ref, wg_ref, wu_ref, wd_ref,
                 gl_ref, o_ref, acc_ref):
    mix_t = jnp.concatenate([_rms_cols(at_ref[...], ga_ref[...]),
                             _rms_cols(bt_ref[...], gb_ref[...])], axis=0).astype(BF16)
    x1 = x_ref[...] + lax.dot_general(mix_t, wo_ref[...], _TN, preferred_element_type=F32)
    h2 = _rms_rows(x1, gf_ref[...]).astype(BF16)
    acc_ref[...] = x1

    def ff_body(c, _):
        gate = jnp.dot(h2, wg_ref[c], preferred_element_type=F32)
        up = jnp.dot(h2, wu_ref[c], preferred_element_type=F32)
        act = (gate * jax.nn.sigmoid(gate) * up).astype(BF16)
        acc_ref[...] += jnp.dot(act, wd_ref[c], preferred_element_type=F32)
        return 0

    lax.fori_loop(0, D_FF // FF_CHUNK, ff_body, 0)
    o_ref[...] = _rms_rows(acc_ref[...], gl_ref[...])


def _post_call(at, bt, x2d, w):
    n = x2d.shape[0]
    tm = TOKEN_TILE
    n_ff = D_FF // FF_CHUNK
    const2 = lambda i: (0, 0)
    const3 = lambda i: (0, 0, 0)
    single = dict(pipeline_mode=pl.Buffered(1))
    return pl.pallas_call(
        _post_kernel,
        grid=(n // tm,),
        in_specs=[
            pl.BlockSpec((MLA_WIDTH, tm), lambda i: (0, i)),
            pl.BlockSpec((NA_WIDTH, tm), lambda i: (0, i)),
            pl.BlockSpec((tm, D_MODEL), lambda i: (i, 0)),
            pl.BlockSpec((MLA_WIDTH, 1), const2),
            pl.BlockSpec((NA_WIDTH, 1), const2),
            pl.BlockSpec((D_MODEL, D_MODEL), const2, **single),
            pl.BlockSpec((1, D_MODEL), const2),
            pl.BlockSpec((n_ff, D_MODEL, FF_CHUNK), const3, **single),
            pl.BlockSpec((n_ff, D_MODEL, FF_CHUNK), const3, **single),
            pl.BlockSpec((n_ff, FF_CHUNK, D_MODEL), const3, **single),
            pl.BlockSpec((1, D_MODEL), const2),
        ],
        out_specs=pl.BlockSpec((tm, D_MODEL), lambda i: (i, 0)),
        out_shape=jax.ShapeDtypeStruct((n, D_MODEL), F32),
        scratch_shapes=[pltpu.VMEM((tm, D_MODEL), F32)],
        compiler_params=pltpu.CompilerParams(
            dimension_semantics=("arbitrary",), vmem_limit_bytes=56 * 1024 * 1024),
        name="post",
    )(at, bt, x2d, w["mla_out_g"], w["na_out_g"], w["w_o"], w["ffn_g"], w["w_gate"], w["w_up"],
      w["w_down"], w["final_g"])


def _pad_heads(wm, width):
    r = wm.shape[0]
    wm = wm.reshape(r, HEADS, width)
    return jnp.pad(wm, ((0, 0), (0, 0), (0, V7X_LANES - width))).reshape(r, HEADS * V7X_LANES)


def _prep_layer(attn_norm_g, w_in, q_norm_g, kv_norm_g, w_uq, w_ukv, na_rpb, mla_out_g, na_out_g,
                w_o, ffn_norm_g, w_gate, w_up, w_down, final_norm_g):
    half = MLA_ROPE_DIM // 2
    o_kr = Q_LORA_RANK + KV_LORA_RANK
    o_na = o_kr + MLA_ROPE_DIM
    w_kr = w_in[:, o_kr:o_na]
    kr_block = jnp.zeros((D_MODEL, V7X_LANES), F32).at[:, HEAD_DIM:HEAD_DIM + MLA_ROPE_DIM].set(w_kr)
    w_naq = w_in[:, o_na:o_na + NA_WIDTH].reshape(D_MODEL, HEADS // 2, 2, HEAD_DIM)
    naq_block = jnp.zeros((D_MODEL, HEADS // 2, 2, 2, HEAD_DIM), F32)
    naq_block = naq_block.at[:, :, 0, 0].set(w_naq[:, :, 0]).at[:, :, 1, 1].set(w_naq[:, :, 1])
    naq_block = naq_block.reshape(D_MODEL, HEADS * V7X_LANES)
    w_nak = w_in[:, o_na + NA_WIDTH:o_na + 2 * NA_WIDTH]
    w_nav = w_in[:, o_na + 2 * NA_WIDTH:]
    w_main = jnp.concatenate([w_in[:, :o_kr], kr_block, naq_block, w_nak], axis=1).astype(BF16)

    w_ukv_h = w_ukv.reshape(KV_LORA_RANK, HEADS, 2 * HEAD_DIM)
    w_k = _pad_heads(w_ukv_h[:, :, :HEAD_DIM].reshape(KV_LORA_RANK, MLA_WIDTH), HEAD_DIM)
    w_v = w_ukv_h[:, :, HEAD_DIM:].reshape(KV_LORA_RANK, MLA_WIDTH)
    n_ff = D_FF // FF_CHUNK
    del half
    return {
        "attn_g": attn_norm_g.reshape(1, D_MODEL),
        "w_main": w_main,
        "q_g": q_norm_g.reshape(1, Q_LORA_RANK),
        "kv_g": kv_norm_g.reshape(1, KV_LORA_RANK),
        "w_uq": _pad_heads(w_uq, MLA_QK_DIM).astype(BF16),
        "w_k": w_k.astype(BF16),
        "w_vt": w_v.T.astype(BF16),
        "w_navt": w_nav.T.astype(BF16),
        "rpb": na_rpb,
        "mla_out_g": mla_out_g.reshape(MLA_WIDTH, 1),
        "na_out_g": na_out_g.reshape(NA_WIDTH, 1),
        "w_o": w_o.astype(BF16),
        "ffn_g": ffn_norm_g.reshape(1, D_MODEL),
        "w_gate": w_gate.reshape(D_MODEL, n_ff, FF_CHUNK).transpose(1, 0, 2).astype(BF16),
        "w_up": w_up.reshape(D_MODEL, n_ff, FF_CHUNK).transpose(1, 0, 2).astype(BF16),
        "w_down": w_down.reshape(n_ff, FF_CHUNK, D_MODEL).astype(BF16),
        "final_g": final_norm_g.reshape(1, D_MODEL),
    }


def _rope_tables(seq):
    half = MLA_ROPE_DIM // 2
    inv = ROPE_THETA ** (-jnp.arange(0, MLA_ROPE_DIM, 2, dtype=F32) / MLA_ROPE_DIM)
    ang = jnp.arange(seq, dtype=F32)[:, None] * inv[None, :]
    cos, sin = jnp.cos(ang), jnp.sin(ang)
    zeros = jnp.zeros((seq, half), F32)
    ones = jnp.ones((seq, HEAD_DIM), F32)
    tail = jnp.zeros((seq, V7X_LANES - MLA_QK_DIM), F32)
    cos_t = jnp.concatenate([ones, cos, cos, tail], axis=1)
    lead = jnp.zeros((seq, HEAD_DIM), F32)
    sin_lo = jnp.concatenate([lead, -sin, zeros, tail], axis=1)
    sin_hi = jnp.concatenate([lead, zeros, sin, tail], axis=1)
    return cos_t, sin_lo, sin_hi


def _na_bias_mask(rpb):
    qc = np.arange(GRID_W)[None, :]
    kc = np.arange(GRID_W)[:, None]
    col_start = np.clip(qc - NA_KW // 2, 0, GRID_W - NA_KW)
    col_ok = (kc >= col_start) & (kc < col_start + NA_KW)
    dj = np.clip(kc - qc, -(NA_KW - 1), NA_KW - 1) + NA_KW - 1
    i = np.arange(NA_CHUNKS * NA_ROWS)[:, None]
    j = np.arange(NA_ROWS)[None, :]
    kinds = [(0, (i < NA_KH) & (j >= 0)),
             (-NA_ROWS, (i - j >= 0) & (i - j < NA_KH)),
             (-2 * NA_ROWS, (i >= NA_ROWS) & (j >= 0))]
    tiles = []
    for off, row_ok in kinds:
        di = np.clip(off + i - j + NA_KH - 1, 0, 2 * NA_KH - 2)
        bias = rpb[:, di[:, None, :, None], dj[None, :, None, :]]
        ok = row_ok[:, None, :, None] & col_ok[None, :, None, :]
        tiles.append(jnp.where(ok[None], bias, NEG_INF))
    bm = jnp.stack(tiles)
    return bm.reshape(3, HEADS, NA_CHUNKS, NA_TOK, NA_TOK)


def _run_trunk(x, w, bias_mask):
    batch, seq, _ = x.shape
    assert seq % TOKEN_TILE == 0 and seq % NA_TOK == 0 and seq // NA_TOK >= NA_CHUNKS
    assert seq % min(seq, MLA_Q_BLOCK) == 0
    x2d = x.reshape(batch * seq, D_MODEL)
    cos_t, sin_lo, sin_hi = _rope_tables(seq)
    wt = dict(w, cos=cos_t, sin_lo=sin_lo, sin_hi=sin_hi)
    q, k, vt, naq, nak, navt = _proj_call(x2d, seq, wt)
    at = _mla_call(q, k, vt, batch, seq)
    bt = _na_call(naq, nak, navt, bias_mask, batch, seq)
    y = _post_call(at, bt, x2d, w)
    return y.reshape(batch, seq, D_MODEL)


def kernel(x_prompt, x_sample, attn_norm_g, w_in, q_norm_g, kv_norm_g, w_uq, w_ukv, na_rpb,
           mla_out_g, na_out_g, w_o, ffn_norm_g, w_gate, w_up, w_down, final_norm_g):
    assert attn_norm_g.shape[0] == 1, "single-layer trunk"
    w = _prep_layer(attn_norm_g[0], w_in[0], q_norm_g[0], kv_norm_g[0], w_uq[0], w_ukv[0], na_rpb[0],
                    mla_out_g[0], na_out_g[0], w_o[0], ffn_norm_g[0], w_gate[0], w_up[0], w_down[0],
                    final_norm_g)
    bias_mask = _na_bias_mask(w["rpb"])
    return (_run_trunk(x_prompt, w, bias_mask), _run_trunk(x_sample, w, bias_mask))
```

```python
import functools
import math

import jax
import jax.numpy as jnp
import numpy as np
from jax import lax
from jax.experimental import pallas as pl
from jax.experimental.pallas import tpu as pltpu

D_MODEL = 1024
MLA_WIDTH = D_MODEL // 2
NA_WIDTH = D_MODEL - MLA_WIDTH
HEADS = 8
HEAD_DIM = 64
MLA_ROPE_DIM = 32
MLA_QK_DIM = HEAD_DIM + MLA_ROPE_DIM
Q_LORA_RANK = 384
KV_LORA_RANK = 256
ROPE_THETA = 10000.0
GRID_W = 64
NA_KH = 8
NA_KW = 16
D_FF = int(math.ceil(8 * D_MODEL / 3 / 256)) * 256
EPS = 1e-6
NEG_INF = -1e30

V7X_LANES = 128
V7X_VMEM_BYTES = 64 * 1024 * 1024

TOKEN_TILE = 512
MLA_Q_TILE = 256
MLA_Q_BLOCK = 1024
NA_ROWS = 4
NA_TOK = NA_ROWS * GRID_W
NA_CHUNKS = 3
FF_CHUNK = 256

_C_Q = 0
_C_KV = _C_Q + Q_LORA_RANK
_C_KR = _C_KV + KV_LORA_RANK
_C_NAQ = _C_KR + V7X_LANES
_C_NAK = _C_NAQ + HEADS * V7X_LANES
_C_END = _C_NAK + NA_WIDTH

_NT = (((1,), (1,)), ((), ()))
_TN = (((0,), (0,)), ((), ()))
BF16 = jnp.bfloat16
F32 = jnp.float32


def _rms_rows(x, g):
    return x * lax.rsqrt(jnp.mean(x * x, axis=-1, keepdims=True) + EPS) * g


def _rms_cols(xt, g):
    return xt * lax.rsqrt(jnp.mean(xt * xt, axis=0, keepdims=True) + EPS) * g


def _rope_lanes(x, cos, sin_lo, sin_hi):
    half = MLA_ROPE_DIM // 2
    return (x * cos + pltpu.roll(x, V7X_LANES - half, 1) * sin_lo
            + pltpu.roll(x, half, 1) * sin_hi)


def _proj_kernel(x_ref, ga_ref, wmain_ref, gq_ref, gkv_ref, wuq_ref, wk_ref, wvt_ref, wnavt_ref,
                 cos_ref, sinlo_ref, sinhi_ref,
                 q_ref, k_ref, vt_ref, naq_ref, nak_ref, navt_ref):
    h = _rms_rows(x_ref[...], ga_ref[...]).astype(BF16)

    def proj(lo, hi):
        return jnp.dot(h, wmain_ref[:, lo:hi], preferred_element_type=F32)

    cqn = _rms_rows(proj(_C_Q, _C_KV), gq_ref[...]).astype(BF16)
    ckvn = _rms_rows(proj(_C_KV, _C_KR), gkv_ref[...]).astype(BF16)
    cos, sin_lo, sin_hi = cos_ref[...], sinlo_ref[...], sinhi_ref[...]
    k_rope = _rope_lanes(proj(_C_KR, _C_NAQ), cos, sin_lo, sin_hi)

    q = jnp.dot(cqn, wuq_ref[...], preferred_element_type=F32)
    kn = jnp.dot(ckvn, wk_ref[...], preferred_element_type=F32)
    vt = lax.dot_general(wvt_ref[...], ckvn, _NT, preferred_element_type=F32)
    naq = proj(_C_NAQ, _C_NAK) * (HEAD_DIM ** -0.5)
    scale = MLA_QK_DIM ** -0.5
    for hd in range(HEADS):
        lanes = slice(hd * V7X_LANES, (hd + 1) * V7X_LANES)
        q_ref[hd] = (_rope_lanes(q[:, lanes], cos, sin_lo, sin_hi) * scale).astype(BF16)
        k_ref[hd] = (kn[:, lanes] + k_rope).astype(BF16)
        vt_ref[hd, 0] = vt[hd * HEAD_DIM:(hd + 1) * HEAD_DIM, :].astype(BF16)
        naq_ref[hd] = naq[:, lanes].astype(BF16)
    nak_ref[...] = proj(_C_NAK, _C_END).astype(BF16)
    navt_ref[...] = lax.dot_general(wnavt_ref[...], h, _NT, preferred_element_type=F32).astype(BF16)


def _proj_call(x2d, seq, w):
    n = x2d.shape[0]
    tm = TOKEN_TILE
    n_tiles = n // tm
    pos_tiles = seq // tm
    const = lambda i: (0, 0)
    single = dict(pipeline_mode=pl.Buffered(1))
    in_specs = [
        pl.BlockSpec((tm, D_MODEL), lambda i: (i, 0)),
        pl.BlockSpec((1, D_MODEL), const),
        pl.BlockSpec((D_MODEL, _C_END), const, **single),
        pl.BlockSpec((1, Q_LORA_RANK), const),
        pl.BlockSpec((1, KV_LORA_RANK), const),
        pl.BlockSpec((Q_LORA_RANK, HEADS * V7X_LANES), const, **single),
        pl.BlockSpec((KV_LORA_RANK, HEADS * V7X_LANES), const, **single),
        pl.BlockSpec((MLA_WIDTH, KV_LORA_RANK), const, **single),
        pl.BlockSpec((NA_WIDTH, D_MODEL), const, **single),
        pl.BlockSpec((tm, V7X_LANES), lambda i: (i % pos_tiles, 0)),
        pl.BlockSpec((tm, V7X_LANES), lambda i: (i % pos_tiles, 0)),
        pl.BlockSpec((tm, V7X_LANES), lambda i: (i % pos_tiles, 0)),
    ]
    out_shape = (
        jax.ShapeDtypeStruct((HEADS, n, V7X_LANES), BF16),
        jax.ShapeDtypeStruct((HEADS, n, V7X_LANES), BF16),
        jax.ShapeDtypeStruct((HEADS, n_tiles, HEAD_DIM, tm), BF16),
        jax.ShapeDtypeStruct((HEADS, n, V7X_LANES), BF16),
        jax.ShapeDtypeStruct((n, NA_WIDTH), BF16),
        jax.ShapeDtypeStruct((NA_WIDTH, n), BF16),
    )
    out_specs = (
        pl.BlockSpec((HEADS, tm, V7X_LANES), lambda i: (0, i, 0)),
        pl.BlockSpec((HEADS, tm, V7X_LANES), lambda i: (0, i, 0)),
        pl.BlockSpec((HEADS, 1, HEAD_DIM, tm), lambda i: (0, i, 0, 0)),
        pl.BlockSpec((HEADS, tm, V7X_LANES), lambda i: (0, i, 0)),
        pl.BlockSpec((tm, NA_WIDTH), lambda i: (i, 0)),
        pl.BlockSpec((NA_WIDTH, tm), lambda i: (0, i)),
    )
    return pl.pallas_call(
        _proj_kernel,
        grid=(n_tiles,),
        in_specs=in_specs,
        out_specs=out_specs,
        out_shape=out_shape,
        compiler_params=pltpu.CompilerParams(
            dimension_semantics=("arbitrary",), vmem_limit_bytes=48 * 1024 * 1024),
        name="proj",
    )(x2d, w["attn_g"], w["w_main"], w["q_g"], w["kv_g"], w["w_uq"], w["w_k"], w["w_vt"],
      w["w_navt"], w["cos"], w["sin_lo"], w["sin_hi"])


def _mla_kernel(q_ref, k_ref, vt_ref, o_ref, acc_ref, *, n_q, n_kv):
    tq, tk = MLA_Q_TILE, TOKEN_TILE
    acc_ref[...] = jnp.zeros_like(acc_ref)

    def kv_body(j, carry):
        ms, ls = carry
        k0 = pl.multiple_of(j * tk, tk)
        k = k_ref[0, pl.ds(k0, tk), :]
        vt = vt_ref[0, j]
        def scores(qi):
            return lax.dot_general(k, q_ref[0, qi * tq:(qi + 1) * tq, :], _NT,
                                   preferred_element_type=F32)

        new_ms, new_ls = [], []
        s_next = scores(0)
        for qi in range(n_q):
            s, s_next = s_next, (scores(qi + 1) if qi + 1 < n_q else None)
            m_new = jnp.maximum(ms[qi], jnp.max(s, axis=0, keepdims=True))
            alpha = jnp.exp(ms[qi] - m_new)
            p = jnp.exp(s - m_new)
            new_ms.append(m_new)
            new_ls.append(alpha * ls[qi] + jnp.sum(p, axis=0, keepdims=True))
            pv = jnp.dot(vt, p.astype(BF16), preferred_element_type=F32)
            acc_ref[qi] = alpha * acc_ref[qi] + pv
        return tuple(new_ms), tuple(new_ls)

    init = (tuple(jnp.full((1, tq), NEG_INF, F32) for _ in range(n_q)),
            tuple(jnp.zeros((1, tq), F32) for _ in range(n_q)))
    _, ls = lax.fori_loop(0, n_kv, kv_body, init)
    for qi in range(n_q):
        o_ref[:, qi * tq:(qi + 1) * tq] = acc_ref[qi] / ls[qi]


def _mla_call(q, k, vt, batch, seq):
    n = q.shape[1]
    qb = min(seq, MLA_Q_BLOCK)
    n_qb = seq // qb
    n_kv = seq // TOKEN_TILE
    n_q = qb // MLA_Q_TILE
    kernel = functools.partial(_mla_kernel, n_q=n_q, n_kv=n_kv)
    return pl.pallas_call(
        kernel,
        grid=(batch, HEADS, n_qb),
        in_specs=[
            pl.BlockSpec((1, qb, V7X_LANES), lambda b, h, i: (h, b * n_qb + i, 0)),
            pl.BlockSpec((1, seq, V7X_LANES), lambda b, h, i: (h, b, 0)),
            pl.BlockSpec((1, n_kv, HEAD_DIM, TOKEN_TILE), lambda b, h, i: (h, b, 0, 0)),
        ],
        out_specs=pl.BlockSpec((HEAD_DIM, qb), lambda b, h, i: (h, b * n_qb + i)),
        out_shape=jax.ShapeDtypeStruct((MLA_WIDTH, n), F32),
        scratch_shapes=[pltpu.VMEM((n_q, HEAD_DIM, MLA_Q_TILE), F32)],
        compiler_params=pltpu.CompilerParams(
            dimension_semantics=("arbitrary", "arbitrary", "arbitrary"),
            vmem_limit_bytes=40 * 1024 * 1024),
        name="mla",
    )(q, k, vt)


def _na_kernel(q_ref, k0_ref, k1_ref, k2_ref, v0_ref, v1_ref, v2_ref, bm_ref, o_ref):
    k_refs = (k0_ref, k1_ref, k2_ref)
    v_refs = (v0_ref, v1_ref, v2_ref)
    for hd in range(HEADS):
        q = q_ref[hd]
        pair = slice((hd // 2) * V7X_LANES, (hd // 2 + 1) * V7X_LANES)
        rows = slice(hd * HEAD_DIM, (hd + 1) * HEAD_DIM)
        s = [lax.dot_general(k_refs[t][:, pair], q, _NT, preferred_element_type=F32)
             + bm_ref[0, hd, t] for t in range(NA_CHUNKS)]
        m = functools.reduce(jnp.maximum, [jnp.max(st, axis=0, keepdims=True) for st in s])
        p = [jnp.exp(st - m) for st in s]
        l = functools.reduce(jnp.add, [jnp.sum(pt, axis=0, keepdims=True) for pt in p])
        o = functools.reduce(jnp.add, [
            jnp.dot(v_refs[t][rows, :], p[t].astype(BF16), preferred_element_type=F32)
            for t in range(NA_CHUNKS)])
        o_ref[rows, :] = o / l


def _na_call(naq, nak, navt, bias_mask, batch, seq):
    n = nak.shape[0]
    groups = seq // NA_TOK

    def win(b, g, t):
        return b * groups + jnp.clip(g - 1, 0, groups - NA_CHUNKS) + t

    def kind(g):
        return jnp.where(g == 0, 0, jnp.where(g == groups - 1, 2, 1))

    k_specs = [pl.BlockSpec((NA_TOK, NA_WIDTH), functools.partial(lambda b, g, t: (win(b, g, t), 0), t=t))
               for t in range(NA_CHUNKS)]
    v_specs = [pl.BlockSpec((NA_WIDTH, NA_TOK), functools.partial(lambda b, g, t: (0, win(b, g, t)), t=t))
               for t in range(NA_CHUNKS)]
    return pl.pallas_call(
        _na_kernel,
        grid=(batch, groups),
        in_specs=[pl.BlockSpec((HEADS, NA_TOK, V7X_LANES), lambda b, g: (0, b * groups + g, 0))]
        + k_specs + v_specs
        + [pl.BlockSpec((1, HEADS, NA_CHUNKS, NA_TOK, NA_TOK), lambda b, g: (kind(g), 0, 0, 0, 0))],
        out_specs=pl.BlockSpec((NA_WIDTH, NA_TOK), lambda b, g: (0, b * groups + g)),
        out_shape=jax.ShapeDtypeStruct((NA_WIDTH, n), F32),
        compiler_params=pltpu.CompilerParams(
            dimension_semantics=("arbitrary", "arbitrary"), vmem_limit_bytes=40 * 1024 * 1024),
        name="na",
    )(naq, nak, nak, nak, navt, navt, navt, bias_mask)


def _post_kernel(at_ref, bt_ref, x_ref, ga_ref, gb_ref, wo_ref, gf_ref, wg_ref, wu_ref, wd_ref,
                 gl_ref, o_ref, acc_ref):
    mix_t = jnp.concatenate([_rms_cols(at_ref[...], ga_ref[...]),
                             _rms_cols(bt_ref[...], gb_ref[...])], axis=0).astype(BF16)
    x1 = x_ref[...] + lax.dot_general(mix_t, wo_ref[...], _TN, preferred_element_type=F32)
    h2 = _rms_rows(x1, gf_ref[...]).astype(BF16)
    acc_ref[...] = x1

    def ff_body(c, _):
        gate = jnp.dot(h2, wg_ref[c], preferred_element_type=F32)
        up = jnp.dot(h2, wu_ref[c], preferred_element_type=F32)
        act = (gate * jax.nn.sigmoid(gate) * up).astype(BF16)
        acc_ref[...] += jnp.dot(act, wd_ref[c], preferred_element_type=F32)
        return 0

    lax.fori_loop(0, D_FF // FF_CHUNK, ff_body, 0)
    o_ref[...] = _rms_rows(acc_ref[...], gl_ref[...])


def _post_call(at, bt, x2d, w):
    n = x2d.shape[0]
    tm = TOKEN_TILE
    n_ff = D_FF // FF_CHUNK
    const2 = lambda i: (0, 0)
    const3 = lambda i: (0, 0, 0)
    single = dict(pipeline_mode=pl.Buffered(1))
    return pl.pallas_call(
        _post_kernel,
        grid=(n // tm,),
        in_specs=[
            pl.BlockSpec((MLA_WIDTH, tm), lambda i: (0, i)),
            pl.BlockSpec((NA_WIDTH, tm), lambda i: (0, i)),
            pl.BlockSpec((tm, D_MODEL), lambda i: (i, 0)),
            pl.BlockSpec((MLA_WIDTH, 1), const2),
            pl.BlockSpec((NA_WIDTH, 1), const2),
            pl.BlockSpec((D_MODEL, D_MODEL), const2, **single),
            pl.BlockSpec((1, D_MODEL), const2),
            pl.BlockSpec((n_ff, D_MODEL, FF_CHUNK), const3, **single),
            pl.BlockSpec((n_ff, D_MODEL, FF_CHUNK), const3, **single),
            pl.BlockSpec((n_ff, FF_CHUNK, D_MODEL), const3, **single),
            pl.BlockSpec((1, D_MODEL), const2),
        ],
        out_specs=pl.BlockSpec((tm, D_MODEL), lambda i: (i, 0)),
        out_shape=jax.ShapeDtypeStruct((n, D_MODEL), F32),
        scratch_shapes=[pltpu.VMEM((tm, D_MODEL), F32)],
        compiler_params=pltpu.CompilerParams(
            dimension_semantics=("arbitrary",), vmem_limit_bytes=56 * 1024 * 1024),
        name="post",
    )(at, bt, x2d, w["mla_out_g"], w["na_out_g"], w["w_o"], w["ffn_g"], w["w_gate"], w["w_up"],
      w["w_down"], w["final_g"])


def _pad_heads(wm, width):
    r = wm.shape[0]
    wm = wm.reshape(r, HEADS, width)
    return jnp.pad(wm, ((0, 0), (0, 0), (0, V7X_LANES - width))).reshape(r, HEADS * V7X_LANES)


def _prep_layer(attn_norm_g, w_in, q_norm_g, kv_norm_g, w_uq, w_ukv, na_rpb, mla_out_g, na_out_g,
                w_o, ffn_norm_g, w_gate, w_up, w_down, final_norm_g):
    half = MLA_ROPE_DIM // 2
    o_kr = Q_LORA_RANK + KV_LORA_RANK
    o_na = o_kr + MLA_ROPE_DIM
    w_kr = w_in[:, o_kr:o_na]
    kr_block = jnp.zeros((D_MODEL, V7X_LANES), F32).at[:, HEAD_DIM:HEAD_DIM + MLA_ROPE_DIM].set(w_kr)
    w_naq = w_in[:, o_na:o_na + NA_WIDTH].reshape(D_MODEL, HEADS // 2, 2, HEAD_DIM)
    naq_block = jnp.zeros((D_MODEL, HEADS // 2, 2, 2, HEAD_DIM), F32)
    naq_block = naq_block.at[:, :, 0, 0].set(w_naq[:, :, 0]).at[:, :, 1, 1].set(w_naq[:, :, 1])
    naq_block = naq_block.reshape(D_MODEL, HEADS * V7X_LANES)
    w_nak = w_in[:, o_na + NA_WIDTH:o_na + 2 * NA_WIDTH]
    w_nav = w_in[:, o_na + 2 * NA_WIDTH:]
    w_main = jnp.concatenate([w_in[:, :o_kr], kr_block, naq_block, w_nak], axis=1).astype(BF16)

    w_ukv_h = w_ukv.reshape(KV_LORA_RANK, HEADS, 2 * HEAD_DIM)
    w_k = _pad_heads(w_ukv_h[:, :, :HEAD_DIM].reshape(KV_LORA_RANK, MLA_WIDTH), HEAD_DIM)
    w_v = w_ukv_h[:, :, HEAD_DIM:].reshape(KV_LORA_RANK, MLA_WIDTH)
    n_ff = D_FF // FF_CHUNK
    del half
    return {
        "attn_g": attn_norm_g.reshape(1, D_MODEL),
        "w_main": w_main,
        "q_g": q_norm_g.reshape(1, Q_LORA_RANK),
        "kv_g": kv_norm_g.reshape(1, KV_LORA_RANK),
        "w_uq": _pad_heads(w_uq, MLA_QK_DIM).astype(BF16),
        "w_k": w_k.astype(BF16),
        "w_vt": w_v.T.astype(BF16),
        "w_navt": w_nav.T.astype(BF16),
        "rpb": na_rpb,
        "mla_out_g": mla_out_g.reshape(MLA_WIDTH, 1),
        "na_out_g": na_out_g.reshape(NA_WIDTH, 1),
        "w_o": w_o.astype(BF16),
        "ffn_g": ffn_norm_g.reshape(1, D_MODEL),
        "w_gate": w_gate.reshape(D_MODEL, n_ff, FF_CHUNK).transpose(1, 0, 2).astype(BF16),
        "w_up": w_up.reshape(D_MODEL, n_ff, FF_CHUNK).transpose(1, 0, 2).astype(BF16),
        "w_down": w_down.reshape(n_ff, FF_CHUNK, D_MODEL).astype(BF16),
        "final_g": final_norm_g.reshape(1, D_MODEL),
    }


def _rope_tables(seq):
    half = MLA_ROPE_DIM // 2
    inv = ROPE_THETA ** (-jnp.arange(0, MLA_ROPE_DIM, 2, dtype=F32) / MLA_ROPE_DIM)
    ang = jnp.arange(seq, dtype=F32)[:, None] * inv[None, :]
    cos, sin = jnp.cos(ang), jnp.sin(ang)
    zeros = jnp.zeros((seq, half), F32)
    ones = jnp.ones((seq, HEAD_DIM), F32)
    tail = jnp.zeros((seq, V7X_LANES - MLA_QK_DIM), F32)
    cos_t = jnp.concatenate([ones, cos, cos, tail], axis=1)
    lead = jnp.zeros((seq, HEAD_DIM), F32)
    sin_lo = jnp.concatenate([lead, -sin, zeros, tail], axis=1)
    sin_hi = jnp.concatenate([lead, zeros, sin, tail], axis=1)
    return cos_t, sin_lo, sin_hi


def _na_bias_mask(rpb):
    qc = np.arange(GRID_W)[None, :]
    kc = np.arange(GRID_W)[:, None]
    col_start = np.clip(qc - NA_KW // 2, 0, GRID_W - NA_KW)
    col_ok = (kc >= col_start) & (kc < col_start + NA_KW)
    period = 2 * GRID_W
    n_d = 2 * NA_KH - 1
    row = jnp.concatenate([rpb[..., NA_KW - 1::-1],
                           jnp.zeros((HEADS, n_d, period - (2 * NA_KW - 1)), F32),
                           rpb[..., :NA_KW - 1:-1]], axis=-1)
    skew = jnp.broadcast_to(row[:, :, None, :], (HEADS, n_d, GRID_W, period))
    skew = skew.reshape(HEADS, n_d, GRID_W * period)[..., :GRID_W * (period - 1)]
    toe = skew.reshape(HEADS, n_d, GRID_W, period - 1)[..., :GRID_W]

    i = np.arange(NA_CHUNKS * NA_ROWS)[:, None]
    j = np.arange(NA_ROWS)[None, :]
    kinds = [(0, (i < NA_KH) & (j >= 0)),
             (-NA_ROWS, (i - j >= 0) & (i - j < NA_KH)),
             (-2 * NA_ROWS, (i >= NA_ROWS) & (j >= 0))]
    di = np.stack([np.clip(off + i - j + NA_KH - 1, 0, n_d - 1) for off, _ in kinds])
    ok = np.stack([r[:, None, :, None] & col_ok[None, :, None, :] for _, r in kinds])
    bias = toe[:, di]
    bias = bias.transpose(1, 0, 2, 4, 3, 5)
    bm = jnp.where(ok[:, None], bias, NEG_INF)
    return bm.reshape(3, HEADS, NA_CHUNKS, NA_TOK, NA_TOK)


def _run_trunk(x, w, bias_mask):
    batch, seq, _ = x.shape
    assert seq % TOKEN_TILE == 0 and seq % NA_TOK == 0 and seq // NA_TOK >= NA_CHUNKS
    assert seq % min(seq, MLA_Q_BLOCK) == 0
    x2d = x.reshape(batch * seq, D_MODEL)
    cos_t, sin_lo, sin_hi = _rope_tables(seq)
    wt = dict(w, cos=cos_t, sin_lo=sin_lo, sin_hi=sin_hi)
    q, k, vt, naq, nak, navt = _proj_call(x2d, seq, wt)
    at = _mla_call(q, k, vt, batch, seq)
    bt = _na_call(naq, nak, navt, bias_mask, batch, seq)
    y = _post_call(at, bt, x2d, w)
    return y.reshape(batch, seq, D_MODEL)


def kernel(x_prompt, x_sample, attn_norm_g, w_in, q_norm_g, kv_norm_g, w_uq, w_ukv, na_rpb,
           mla_out_g, na_out_g, w_o, ffn_norm_g, w_gate, w_up, w_down, final_norm_g):
    assert attn_norm_g.shape[0] == 1, "single-layer trunk"
    w = _prep_layer(attn_norm_g[0], w_in[0], q_norm_g[0], kv_norm_g[0], w_uq[0], w_ukv[0], na_rpb[0],
                    mla_out_g[0], na_out_g[0], w_o[0], ffn_norm_g[0], w_gate[0], w_up[0], w_down[0],
                    final_norm_g)
    bias_mask = _na_bias_mask(w["rpb"])
    return (_run_trunk(x_prompt, w, bias_mask), _run_trunk(x_sample, w, bias_mask))
```

```python
import functools
import math

import jax
import jax.numpy as jnp
import numpy as np
from jax import lax
from jax.experimental import pallas as pl
from jax.experimental.pallas import tpu as pltpu

D_MODEL = 1024
MLA_WIDTH = D_MODEL // 2
NA_WIDTH = D_MODEL - MLA_WIDTH
HEADS = 8
HEAD_DIM = 64
MLA_ROPE_DIM = 32
MLA_QK_DIM = HEAD_DIM + MLA_ROPE_DIM
Q_LORA_RANK = 384
KV_LORA_RANK = 256
ROPE_THETA = 10000.0
GRID_W = 64
NA_KH = 8
NA_KW = 16
D_FF = int(math.ceil(8 * D_MODEL / 3 / 256)) * 256
EPS = 1e-6
NEG_INF = -1e30
LOG2_E = math.log2(math.e)
V_ROWS = HEAD_DIM + 16

V7X_LANES = 128
V7X_VMEM_BYTES = 64 * 1024 * 1024

TOKEN_TILE = 512
MLA_Q_TILE = 256
MLA_Q_BLOCK = 1024
NA_ROWS = 4
NA_TOK = NA_ROWS * GRID_W
NA_CHUNKS = 3
FF_CHUNK = 256

_C_Q = 0
_C_KV = _C_Q + Q_LORA_RANK
_C_KR = _C_KV + KV_LORA_RANK
_C_NAQ = _C_KR + V7X_LANES
_C_NAK = _C_NAQ + HEADS * V7X_LANES
_C_END = _C_NAK + NA_WIDTH

_NT = (((1,), (1,)), ((), ()))
_TN = (((0,), (0,)), ((), ()))
BF16 = jnp.bfloat16
F32 = jnp.float32


def _rms_rows(x, g):
    return x * lax.rsqrt(jnp.mean(x * x, axis=-1, keepdims=True) + EPS) * g


def _rms_cols(xt, g):
    return xt * lax.rsqrt(jnp.mean(xt * xt, axis=0, keepdims=True) + EPS) * g


def _rope_lanes(x, cos, sin_lo, sin_hi):
    half = MLA_ROPE_DIM // 2
    return (x * cos + pltpu.roll(x, V7X_LANES - half, 1) * sin_lo
            + pltpu.roll(x, half, 1) * sin_hi)


def _proj_kernel(x_ref, ga_ref, wmain_ref, gq_ref, gkv_ref, wuq_ref, wk_ref, wvt_ref, wnavt_ref,
                 cos_ref, sinlo_ref, sinhi_ref,
                 q_ref, k_ref, vt_ref, naq_ref, nak_ref, navt_ref):
    h = _rms_rows(x_ref[...], ga_ref[...]).astype(BF16)

    def proj(lo, hi):
        return jnp.dot(h, wmain_ref[:, lo:hi], preferred_element_type=F32)

    cqn = _rms_rows(proj(_C_Q, _C_KV), gq_ref[...]).astype(BF16)
    ckvn = _rms_rows(proj(_C_KV, _C_KR), gkv_ref[...]).astype(BF16)
    cos, sin_lo, sin_hi = cos_ref[...], sinlo_ref[...], sinhi_ref[...]
    k_rope = _rope_lanes(proj(_C_KR, _C_NAQ), cos, sin_lo, sin_hi)

    q = jnp.dot(cqn, wuq_ref[...], preferred_element_type=F32)
    kn = jnp.dot(ckvn, wk_ref[...], preferred_element_type=F32)
    vt = lax.dot_general(wvt_ref[...], ckvn, _NT, preferred_element_type=F32)
    naq = proj(_C_NAQ, _C_NAK) * (HEAD_DIM ** -0.5 * LOG2_E)
    scale = MLA_QK_DIM ** -0.5 * LOG2_E
    ones = jnp.ones((V_ROWS - HEAD_DIM, vt.shape[1]), BF16)
    for hd in range(HEADS):
        lanes = slice(hd * V7X_LANES, (hd + 1) * V7X_LANES)
        q_ref[hd] = (_rope_lanes(q[:, lanes], cos, sin_lo, sin_hi) * scale).astype(BF16)
        k_ref[hd] = (kn[:, lanes] + k_rope).astype(BF16)
        vt_ref[hd, 0, :HEAD_DIM, :] = vt[hd * HEAD_DIM:(hd + 1) * HEAD_DIM, :].astype(BF16)
        vt_ref[hd, 0, HEAD_DIM:, :] = ones
        naq_ref[hd] = naq[:, lanes].astype(BF16)
    nak_ref[...] = proj(_C_NAK, _C_END).astype(BF16)
    navt_ref[...] = lax.dot_general(wnavt_ref[...], h, _NT, preferred_element_type=F32).astype(BF16)


def _proj_call(x2d, seq, w):
    n = x2d.shape[0]
    tm = TOKEN_TILE
    n_tiles = n // tm
    pos_tiles = seq // tm
    const = lambda i: (0, 0)
    single = dict(pipeline_mode=pl.Buffered(1))
    in_specs = [
        pl.BlockSpec((tm, D_MODEL), lambda i: (i, 0)),
        pl.BlockSpec((1, D_MODEL), const),
        pl.BlockSpec((D_MODEL, _C_END), const, **single),
        pl.BlockSpec((1, Q_LORA_RANK), const),
        pl.BlockSpec((1, KV_LORA_RANK), const),
        pl.BlockSpec((Q_LORA_RANK, HEADS * V7X_LANES), const, **single),
        pl.BlockSpec((KV_LORA_RANK, HEADS * V7X_LANES), const, **single),
        pl.BlockSpec((MLA_WIDTH, KV_LORA_RANK), const, **single),
        pl.BlockSpec((NA_WIDTH, D_MODEL), const, **single),
        pl.BlockSpec((tm, V7X_LANES), lambda i: (i % pos_tiles, 0)),
        pl.BlockSpec((tm, V7X_LANES), lambda i: (i % pos_tiles, 0)),
        pl.BlockSpec((tm, V7X_LANES), lambda i: (i % pos_tiles, 0)),
    ]
    out_shape = (
        jax.ShapeDtypeStruct((HEADS, n, V7X_LANES), BF16),
        jax.ShapeDtypeStruct((HEADS, n, V7X_LANES), BF16),
        jax.ShapeDtypeStruct((HEADS, n_tiles, V_ROWS, tm), BF16),
        jax.ShapeDtypeStruct((HEADS, n, V7X_LANES), BF16),
        jax.ShapeDtypeStruct((n, NA_WIDTH), BF16),
        jax.ShapeDtypeStruct((NA_WIDTH, n), BF16),
    )
    out_specs = (
        pl.BlockSpec((HEADS, tm, V7X_LANES), lambda i: (0, i, 0)),
        pl.BlockSpec((HEADS, tm, V7X_LANES), lambda i: (0, i, 0)),
        pl.BlockSpec((HEADS, 1, V_ROWS, tm), lambda i: (0, i, 0, 0)),
        pl.BlockSpec((HEADS, tm, V7X_LANES), lambda i: (0, i, 0)),
        pl.BlockSpec((tm, NA_WIDTH), lambda i: (i, 0)),
        pl.BlockSpec((NA_WIDTH, tm), lambda i: (0, i)),
    )
    return pl.pallas_call(
        _proj_kernel,
        grid=(n_tiles,),
        in_specs=in_specs,
        out_specs=out_specs,
        out_shape=out_shape,
        compiler_params=pltpu.CompilerParams(
            dimension_semantics=("arbitrary",), vmem_limit_bytes=48 * 1024 * 1024),
        name="proj",
    )(x2d, w["attn_g"], w["w_main"], w["q_g"], w["kv_g"], w["w_uq"], w["w_k"], w["w_vt"],
      w["w_navt"], w["cos"], w["sin_lo"], w["sin_hi"])


def _mla_kernel(q_ref, k_ref, vt_ref, o_ref, acc_ref, s_ref, *, n_q, n_kv):
    tq, tk = MLA_Q_TILE, TOKEN_TILE
    acc_ref[...] = jnp.zeros_like(acc_ref)

    def produce(j, qi, slot):
        k0 = pl.multiple_of(j * tk, tk)
        s = lax.dot_general(k_ref[0, pl.ds(k0, tk), :], q_ref[0, qi * tq:(qi + 1) * tq, :], _NT,
                            preferred_element_type=F32)
        s_ref[slot, qi] = s
        return jnp.max(s, axis=0, keepdims=True)

    def consume(j, qi, slot, m, cmax):
        m_new = jnp.maximum(m, cmax)
        alpha = jnp.exp2(m - m_new)
        p = jnp.exp2(s_ref[slot, qi] - m_new).astype(BF16)
        acc_ref[qi] = alpha * acc_ref[qi] + jnp.dot(vt_ref[0, j], p, preferred_element_type=F32)
        return m_new

    def half_step(j, slot, ms, cmaxs, prefetch):
        new_ms, new_cmaxs = [], []
        for qi in range(n_q):
            if prefetch:
                new_cmaxs.append(produce(j + 1, qi, 1 - slot))
            new_ms.append(consume(j, qi, slot, ms[qi], cmaxs[qi]))
        return tuple(new_ms), tuple(new_cmaxs)

    def body(jj, carry):
        ms, cmaxs = half_step(2 * jj, 0, *carry, True)
        return half_step(2 * jj + 1, 1, ms, cmaxs, True)

    ms = tuple(jnp.full((1, tq), NEG_INF, F32) for _ in range(n_q))
    cmaxs = tuple(produce(0, qi, 0) for qi in range(n_q))
    ms, cmaxs = lax.fori_loop(0, n_kv // 2 - 1, body, (ms, cmaxs))
    ms, cmaxs = half_step(n_kv - 2, 0, ms, cmaxs, True)
    half_step(n_kv - 1, 1, ms, cmaxs, False)
    for qi in range(n_q):
        o_ref[:, qi * tq:(qi + 1) * tq] = (acc_ref[qi, :HEAD_DIM, :]
                                           / acc_ref[qi, HEAD_DIM:HEAD_DIM + 1, :])


def _mla_call(q, k, vt, batch, seq):
    n = q.shape[1]
    qb = min(seq, MLA_Q_BLOCK)
    n_qb = seq // qb
    n_kv = seq // TOKEN_TILE
    n_q = qb // MLA_Q_TILE
    assert n_kv % 2 == 0
    kernel = functools.partial(_mla_kernel, n_q=n_q, n_kv=n_kv)
    return pl.pallas_call(
        kernel,
        grid=(batch, HEADS, n_qb),
        in_specs=[
            pl.BlockSpec((1, qb, V7X_LANES), lambda b, h, i: (h, b * n_qb + i, 0)),
            pl.BlockSpec((1, seq, V7X_LANES), lambda b, h, i: (h, b, 0)),
            pl.BlockSpec((1, n_kv, V_ROWS, TOKEN_TILE), lambda b, h, i: (h, b, 0, 0)),
        ],
        out_specs=pl.BlockSpec((HEAD_DIM, qb), lambda b, h, i: (h, b * n_qb + i)),
        out_shape=jax.ShapeDtypeStruct((MLA_WIDTH, n), F32),
        scratch_shapes=[pltpu.VMEM((n_q, V_ROWS, MLA_Q_TILE), F32),
                        pltpu.VMEM((2, n_q, TOKEN_TILE, MLA_Q_TILE), F32)],
        compiler_params=pltpu.CompilerParams(
            dimension_semantics=("arbitrary", "arbitrary", "arbitrary"),
            vmem_limit_bytes=40 * 1024 * 1024),
        name="mla",
    )(q, k, vt)


def _na_kernel(q_ref, k0_ref, k1_ref, k2_ref, v0_ref, v1_ref, v2_ref, bm_ref, o_ref, s_ref):
    k_refs = (k0_ref, k1_ref, k2_ref)
    v_refs = (v0_ref, v1_ref, v2_ref)

    def produce(hd, slot):
        q = q_ref[hd]
        pair = slice((hd // 2) * V7X_LANES, (hd // 2 + 1) * V7X_LANES)
        col_max = []
        for t in range(NA_CHUNKS):
            s = (lax.dot_general(k_refs[t][:, pair], q, _NT, preferred_element_type=F32)
                 + bm_ref[0, hd, t])
            s_ref[slot, t] = s
            col_max.append(jnp.max(s, axis=0, keepdims=True))
        return functools.reduce(jnp.maximum, col_max)

    def consume(hd, slot, m):
        rows = slice(hd * HEAD_DIM, (hd + 1) * HEAD_DIM)
        l, o = 0.0, 0.0
        for t in range(NA_CHUNKS):
            p = jnp.exp2(s_ref[slot, t] - m)
            l = l + jnp.sum(p, axis=0, keepdims=True)
            o = o + jnp.dot(v_refs[t][rows, :], p.astype(BF16), preferred_element_type=F32)
        o_ref[rows, :] = o / l

    m = produce(0, 0)
    for hd in range(HEADS):
        m_next = produce(hd + 1, (hd + 1) % 2) if hd + 1 < HEADS else None
        consume(hd, hd % 2, m)
        m = m_next


def _na_call(naq, nak, navt, bias_mask, batch, seq):
    n = nak.shape[0]
    groups = seq // NA_TOK

    def win(b, g, t):
        return b * groups + jnp.clip(g - 1, 0, groups - NA_CHUNKS) + t

    def kind(g):
        return jnp.where(g == 0, 0, jnp.where(g == groups - 1, 2, 1))

    k_specs = [pl.BlockSpec((NA_TOK, NA_WIDTH), functools.partial(lambda b, g, t: (win(b, g, t), 0), t=t))
               for t in range(NA_CHUNKS)]
    v_specs = [pl.BlockSpec((NA_WIDTH, NA_TOK), functools.partial(lambda b, g, t: (0, win(b, g, t)), t=t))
               for t in range(NA_CHUNKS)]
    return pl.pallas_call(
        _na_kernel,
        grid=(batch, groups),
        in_specs=[pl.BlockSpec((HEADS, NA_TOK, V7X_LANES), lambda b, g: (0, b * groups + g, 0))]
        + k_specs + v_specs
        + [pl.BlockSpec((1, HEADS, NA_CHUNKS, NA_TOK, NA_TOK), lambda b, g: (kind(g), 0, 0, 0, 0))],
        out_specs=pl.BlockSpec((NA_WIDTH, NA_TOK), lambda b, g: (0, b * groups + g)),
        out_shape=jax.ShapeDtypeStruct((NA_WIDTH, n), F32),
        scratch_shapes=[pltpu.VMEM((2, NA_CHUNKS, NA_TOK, NA_TOK), F32)],
        compiler_params=pltpu.CompilerParams(
            dimension_semantics=("arbitrary", "arbitrary"), vmem_limit_bytes=40 * 1024 * 1024),
        name="na",
    )(naq, nak, nak, nak, navt, navt, navt, bias_mask)


def _post_kernel(at_ref, bt_ref, x_ref, ga_ref, gb_ref, wo_ref, gf_ref, wg_ref, wu_ref, wd_ref,
                 gl_ref, o_ref, acc_ref):
    mix_t = jnp.concatenate([_rms_cols(at_ref[...], ga_ref[...]),
                             _rms_cols(bt_ref[...], gb_ref[...])], axis=0).astype(BF16)
    x1 = x_ref[...] + lax.dot_general(mix_t, wo_ref[...], _TN, preferred_element_type=F32)
    h2 = _rms_rows(x1, gf_ref[...]).astype(BF16)
    acc_ref[...] = x1

    def ff_body(c, _):
        gate = jnp.dot(h2, wg_ref[c], preferred_element_type=F32)
        up = jnp.dot(h2, wu_ref[c], preferred_element_type=F32)
        act = (gate * jax.nn.sigmoid(gate) * up).astype(BF16)
        acc_ref[...] += jnp.dot(act, wd_ref[c], preferred_element_type=F32)
        return 0

    lax.fori_loop(0, D_FF // FF_CHUNK, ff_body, 0)
    o_ref[...] = _rms_rows(acc_ref[...], gl_ref[...])


def _post_call(at, bt, x2d, w):
    n = x2d.shape[0]
    tm = TOKEN_TILE
    n_ff = D_FF // FF_CHUNK
    const2 = lambda i: (0, 0)
    const3 = lambda i: (0, 0, 0)
    single = dict(pipeline_mode=pl.Buffered(1))
    return pl.pallas_call(
        _post_kernel,
        grid=(n // tm,),
        in_specs=[
            pl.BlockSpec((MLA_WIDTH, tm), lambda i: (0, i)),
            pl.BlockSpec((NA_WIDTH, tm), lambda i: (0, i)),
            pl.BlockSpec((tm, D_MODEL), lambda i: (i, 0)),
            pl.BlockSpec((MLA_WIDTH, 1), const2),
            pl.BlockSpec((NA_WIDTH, 1), const2),
            pl.BlockSpec((D_MODEL, D_MODEL), const2, **single),
            pl.BlockSpec((1, D_MODEL), const2),
            pl.BlockSpec((n_ff, D_MODEL, FF_CHUNK), const3, **single),
            pl.BlockSpec((n_ff, D_MODEL, FF_CHUNK), const3, **single),
            pl.BlockSpec((n_ff, FF_CHUNK, D_MODEL), const3, **single),
            pl.BlockSpec((1, D_MODEL), const2),
        ],
        out_specs=pl.BlockSpec((tm, D_MODEL), lambda i: (i, 0)),
        out_shape=jax.ShapeDtypeStruct((n, D_MODEL), F32),
        scratch_shapes=[pltpu.VMEM((tm, D_MODEL), F32)],
        compiler_params=pltpu.CompilerParams(
            dimension_semantics=("arbitrary",), vmem_limit_bytes=56 * 1024 * 1024),
        name="post",
    )(at, bt, x2d, w["mla_out_g"], w["na_out_g"], w["w_o"], w["ffn_g"], w["w_gate"], w["w_up"],
      w["w_down"], w["final_g"])


def _pad_heads(wm, width):
    r = wm.shape[0]
    wm = wm.reshape(r, HEADS, width)
    return jnp.pad(wm, ((0, 0), (0, 0), (0, V7X_LANES - width))).reshape(r, HEADS * V7X_LANES)


def _prep_layer(attn_norm_g, w_in, q_norm_g, kv_norm_g, w_uq, w_ukv, na_rpb, mla_out_g, na_out_g,
                w_o, ffn_norm_g, w_gate, w_up, w_down, final_norm_g):
    half = MLA_ROPE_DIM // 2
    o_kr = Q_LORA_RANK + KV_LORA_RANK
    o_na = o_kr + MLA_ROPE_DIM
    w_kr = w_in[:, o_kr:o_na]
    kr_block = jnp.zeros((D_MODEL, V7X_LANES), F32).at[:, HEAD_DIM:HEAD_DIM + MLA_ROPE_DIM].set(w_kr)
    w_naq = w_in[:, o_na:o_na + NA_WIDTH].reshape(D_MODEL, HEADS // 2, 2, HEAD_DIM)
    naq_block = jnp.zeros((D_MODEL, HEADS // 2, 2, 2, HEAD_DIM), F32)
    naq_block = naq_block.at[:, :, 0, 0].set(w_naq[:, :, 0]).at[:, :, 1, 1].set(w_naq[:, :, 1])
    naq_block = naq_block.reshape(D_MODEL, HEADS * V7X_LANES)
    w_nak = w_in[:, o_na + NA_WIDTH:o_na + 2 * NA_WIDTH]
    w_nav = w_in[:, o_na + 2 * NA_WIDTH:]
    w_main = jnp.concatenate([w_in[:, :o_kr], kr_block, naq_block, w_nak], axis=1).astype(BF16)

    w_ukv_h = w_ukv.reshape(KV_LORA_RANK, HEADS, 2 * HEAD_DIM)
    w_k = _pad_heads(w_ukv_h[:, :, :HEAD_DIM].reshape(KV_LORA_RANK, MLA_WIDTH), HEAD_DIM)
    w_v = w_ukv_h[:, :, HEAD_DIM:].reshape(KV_LORA_RANK, MLA_WIDTH)
    n_ff = D_FF // FF_CHUNK
    del half
    return {
        "attn_g": attn_norm_g.reshape(1, D_MODEL),
        "w_main": w_main,
        "q_g": q_norm_g.reshape(1, Q_LORA_RANK),
        "kv_g": kv_norm_g.reshape(1, KV_LORA_RANK),
        "w_uq": _pad_heads(w_uq, MLA_QK_DIM).astype(BF16),
        "w_k": w_k.astype(BF16),
        "w_vt": w_v.T.astype(BF16),
        "w_navt": w_nav.T.astype(BF16),
        "rpb": na_rpb,
        "mla_out_g": mla_out_g.reshape(MLA_WIDTH, 1),
        "na_out_g": na_out_g.reshape(NA_WIDTH, 1),
        "w_o": w_o.astype(BF16),
        "ffn_g": ffn_norm_g.reshape(1, D_MODEL),
        "w_gate": w_gate.reshape(D_MODEL, n_ff, FF_CHUNK).transpose(1, 0, 2).astype(BF16),
        "w_up": w_up.reshape(D_MODEL, n_ff, FF_CHUNK).transpose(1, 0, 2).astype(BF16),
        "w_down": w_down.reshape(n_ff, FF_CHUNK, D_MODEL).astype(BF16),
        "final_g": final_norm_g.reshape(1, D_MODEL),
    }


def _rope_tables(seq):
    half = MLA_ROPE_DIM // 2
    inv = ROPE_THETA ** (-jnp.arange(0, MLA_ROPE_DIM, 2, dtype=F32) / MLA_ROPE_DIM)
    ang = jnp.arange(seq, dtype=F32)[:, None] * inv[None, :]
    cos, sin = jnp.cos(ang), jnp.sin(ang)
    zeros = jnp.zeros((seq, half), F32)
    ones = jnp.ones((seq, HEAD_DIM), F32)
    tail = jnp.zeros((seq, V7X_LANES - MLA_QK_DIM), F32)
    cos_t = jnp.concatenate([ones, cos, cos, tail], axis=1)
    lead = jnp.zeros((seq, HEAD_DIM), F32)
    sin_lo = jnp.concatenate([lead, -sin, zeros, tail], axis=1)
    sin_hi = jnp.concatenate([lead, zeros, sin, tail], axis=1)
    return cos_t, sin_lo, sin_hi


def _na_bias_mask(rpb):
    qc = np.arange(GRID_W)[None, :]
    kc = np.arange(GRID_W)[:, None]
    col_start = np.clip(qc - NA_KW // 2, 0, GRID_W - NA_KW)
    col_ok = (kc >= col_start) & (kc < col_start + NA_KW)
    period = 2 * GRID_W
    n_d = 2 * NA_KH - 1
    row = jnp.concatenate([rpb[..., NA_KW - 1::-1],
                           jnp.zeros((HEADS, n_d, period - (2 * NA_KW - 1)), F32),
                           rpb[..., :NA_KW - 1:-1]], axis=-1)
    skew = jnp.broadcast_to(row[:, :, None, :], (HEADS, n_d, GRID_W, period))
    skew = skew.reshape(HEADS, n_d, GRID_W * period)[..., :GRID_W * (period - 1)]
    toe = skew.reshape(HEADS, n_d, GRID_W, period - 1)[..., :GRID_W]

    i = np.arange(NA_CHUNKS * NA_ROWS)[:, None]
    j = np.arange(NA_ROWS)[None, :]
    kinds = [(0, (i < NA_KH) & (j >= 0)),
             (-NA_ROWS, (i - j >= 0) & (i - j < NA_KH)),
             (-2 * NA_ROWS, (i >= NA_ROWS) & (j >= 0))]
    di = np.stack([np.clip(off + i - j + NA_KH - 1, 0, n_d - 1) for off, _ in kinds])
    ok = np.stack([r[:, None, :, None] & col_ok[None, :, None, :] for _, r in kinds])
    bias = toe[:, di]
    bias = bias.transpose(1, 0, 2, 4, 3, 5)
    bm = jnp.where(ok[:, None], bias * LOG2_E, NEG_INF)
    return bm.reshape(3, HEADS, NA_CHUNKS, NA_TOK, NA_TOK)


def _run_trunk(x, w, bias_mask):
    batch, seq, _ = x.shape
    assert seq % TOKEN_TILE == 0 and seq % NA_TOK == 0 and seq // NA_TOK >= NA_CHUNKS
    assert seq % min(seq, MLA_Q_BLOCK) == 0
    x2d = x.reshape(batch * seq, D_MODEL)
    cos_t, sin_lo, sin_hi = _rope_tables(seq)
    wt = dict(w, cos=cos_t, sin_lo=sin_lo, sin_hi=sin_hi)
    q, k, vt, naq, nak, navt = _proj_call(x2d, seq, wt)
    at = _mla_call(q, k, vt, batch, seq)
    bt = _na_call(naq, nak, navt, bias_mask, batch, seq)
    y = _post_call(at, bt, x2d, w)
    return y.reshape(batch, seq, D_MODEL)


def kernel(x_prompt, x_sample, attn_norm_g, w_in, q_norm_g, kv_norm_g, w_uq, w_ukv, na_rpb,
           mla_out_g, na_out_g, w_o, ffn_norm_g, w_gate, w_up, w_down, final_norm_g):
    assert attn_norm_g.shape[0] == 1, "single-layer trunk"
    w = _prep_layer(attn_norm_g[0], w_in[0], q_norm_g[0], kv_norm_g[0], w_uq[0], w_ukv[0], na_rpb[0],
                    mla_out_g[0], na_out_g[0], w_o[0], ffn_norm_g[0], w_gate[0], w_up[0], w_down[0],
                    final_norm_g)
    bias_mask = _na_bias_mask(w["rpb"])
    return (_run_trunk(x_prompt, w, bias_mask), _run_trunk(x_sample, w, bias_mask))
```

```python
import functools
import math

import jax
import jax.numpy as jnp
import numpy as np
from jax import lax
from jax.experimental import pallas as pl
from jax.experimental.pallas import tpu as pltpu

D_MODEL = 1024
MLA_WIDTH = D_MODEL // 2
NA_WIDTH = D_MODEL - MLA_WIDTH
HEADS = 8
HEAD_DIM = 64
MLA_ROPE_DIM = 32
MLA_QK_DIM = HEAD_DIM + MLA_ROPE_DIM
Q_LORA_RANK = 384
KV_LORA_RANK = 256
ROPE_THETA = 10000.0
GRID_W = 64
NA_KH = 8
NA_KW = 16
D_FF = int(math.ceil(8 * D_MODEL / 3 / 256)) * 256
EPS = 1e-6
NEG_INF = -1e30
LOG2_E = math.log2(math.e)
V_ROWS = HEAD_DIM + 16

V7X_LANES = 128
V7X_VMEM_BYTES = 64 * 1024 * 1024

TOKEN_TILE = 512
MLA_Q_TILE = 256
MLA_Q_BLOCK = 2048
MLA_KV_GROUP = 4
MLA_MIN_SWEEPS = 4
NA_ROWS = 4
NA_TOK = NA_ROWS * GRID_W
NA_CHUNKS = 3
FF_CHUNK = 256

_C_Q = 0
_C_KV = _C_Q + Q_LORA_RANK
_C_KR = _C_KV + KV_LORA_RANK
_C_NAQ = _C_KR + V7X_LANES
_C_NAK = _C_NAQ + NA_WIDTH
_C_END = _C_NAK + NA_WIDTH

_NT = (((1,), (1,)), ((), ()))
_TN = (((0,), (0,)), ((), ()))
BF16 = jnp.bfloat16
F32 = jnp.float32


def _rms_rows(x, g):
    return x * lax.rsqrt(jnp.mean(x * x, axis=-1, keepdims=True) + EPS) * g


def _rms_cols(xt, g):
    return xt * lax.rsqrt(jnp.mean(xt * xt, axis=0, keepdims=True) + EPS) * g


def _rope_lanes(x, cos, sin_lo, sin_hi):
    half = MLA_ROPE_DIM // 2
    return (x * cos + pltpu.roll(x, V7X_LANES - half, 1) * sin_lo
            + pltpu.roll(x, half, 1) * sin_hi)


def _proj_kernel(x_ref, ga_ref, wmain_ref, gq_ref, gkv_ref, wuq_ref, wk_ref, wvt_ref, wnavt_ref,
                 cos_ref, sinlo_ref, sinhi_ref,
                 q_ref, k_ref, vt_ref, naq_ref, nak_ref, navt_ref):
    h = _rms_rows(x_ref[...], ga_ref[...]).astype(BF16)

    def proj(lo, hi):
        return jnp.dot(h, wmain_ref[:, lo:hi], preferred_element_type=F32)

    cqn = _rms_rows(proj(_C_Q, _C_KV), gq_ref[...]).astype(BF16)
    ckvn = _rms_rows(proj(_C_KV, _C_KR), gkv_ref[...]).astype(BF16)
    cos, sin_lo, sin_hi = cos_ref[...], sinlo_ref[...], sinhi_ref[...]
    k_rope = _rope_lanes(proj(_C_KR, _C_NAQ), cos, sin_lo, sin_hi)

    q = jnp.dot(cqn, wuq_ref[...], preferred_element_type=F32)
    kn = jnp.dot(ckvn, wk_ref[...], preferred_element_type=F32)
    vt = lax.dot_general(wvt_ref[...], ckvn, _NT, preferred_element_type=F32)
    naq = proj(_C_NAQ, _C_NAK) * (HEAD_DIM ** -0.5 * LOG2_E)
    scale = MLA_QK_DIM ** -0.5 * LOG2_E
    ones = jnp.ones((V_ROWS - HEAD_DIM, vt.shape[1]), BF16)
    lane_half = lax.broadcasted_iota(jnp.int32, (1, V7X_LANES), 1) // HEAD_DIM
    for hd in range(HEADS):
        lanes = slice(hd * V7X_LANES, (hd + 1) * V7X_LANES)
        q_ref[hd] = (_rope_lanes(q[:, lanes], cos, sin_lo, sin_hi) * scale).astype(BF16)
        k_ref[hd] = (kn[:, lanes] + k_rope).astype(BF16)
        vt_ref[hd, 0, :HEAD_DIM, :] = vt[hd * HEAD_DIM:(hd + 1) * HEAD_DIM, :].astype(BF16)
        vt_ref[hd, 0, HEAD_DIM:, :] = ones
        pair = naq[:, (hd // 2) * V7X_LANES:(hd // 2 + 1) * V7X_LANES]
        naq_ref[hd] = jnp.where(lane_half == hd % 2, pair, 0.0).astype(BF16)
    nak_ref[...] = proj(_C_NAK, _C_END).astype(BF16)
    navt_ref[...] = lax.dot_general(wnavt_ref[...], h, _NT, preferred_element_type=F32).astype(BF16)


def _proj_call(x2d, seq, w):
    n = x2d.shape[0]
    tm = TOKEN_TILE
    n_tiles = n // tm
    pos_tiles = seq // tm
    const = lambda i: (0, 0)
    single = dict(pipeline_mode=pl.Buffered(1))
    in_specs = [
        pl.BlockSpec((tm, D_MODEL), lambda i: (i, 0)),
        pl.BlockSpec((1, D_MODEL), const),
        pl.BlockSpec((D_MODEL, _C_END), const, **single),
        pl.BlockSpec((1, Q_LORA_RANK), const),
        pl.BlockSpec((1, KV_LORA_RANK), const),
        pl.BlockSpec((Q_LORA_RANK, HEADS * V7X_LANES), const, **single),
        pl.BlockSpec((KV_LORA_RANK, HEADS * V7X_LANES), const, **single),
        pl.BlockSpec((MLA_WIDTH, KV_LORA_RANK), const, **single),
        pl.BlockSpec((NA_WIDTH, D_MODEL), const, **single),
        pl.BlockSpec((tm, V7X_LANES), lambda i: (i % pos_tiles, 0)),
        pl.BlockSpec((tm, V7X_LANES), lambda i: (i % pos_tiles, 0)),
        pl.BlockSpec((tm, V7X_LANES), lambda i: (i % pos_tiles, 0)),
    ]
    out_shape = (
        jax.ShapeDtypeStruct((HEADS, n, V7X_LANES), BF16),
        jax.ShapeDtypeStruct((HEADS, n, V7X_LANES), BF16),
        jax.ShapeDtypeStruct((HEADS, n_tiles, V_ROWS, tm), BF16),
        jax.ShapeDtypeStruct((HEADS, n, V7X_LANES), BF16),
        jax.ShapeDtypeStruct((n, NA_WIDTH), BF16),
        jax.ShapeDtypeStruct((NA_WIDTH, n), BF16),
    )
    out_specs = (
        pl.BlockSpec((HEADS, tm, V7X_LANES), lambda i: (0, i, 0)),
        pl.BlockSpec((HEADS, tm, V7X_LANES), lambda i: (0, i, 0)),
        pl.BlockSpec((HEADS, 1, V_ROWS, tm), lambda i: (0, i, 0, 0)),
        pl.BlockSpec((HEADS, tm, V7X_LANES), lambda i: (0, i, 0)),
        pl.BlockSpec((tm, NA_WIDTH), lambda i: (i, 0)),
        pl.BlockSpec((NA_WIDTH, tm), lambda i: (0, i)),
    )
    return pl.pallas_call(
        _proj_kernel,
        grid=(n_tiles,),
        in_specs=in_specs,
        out_specs=out_specs,
        out_shape=out_shape,
        compiler_params=pltpu.CompilerParams(
            dimension_semantics=("arbitrary",), vmem_limit_bytes=48 * 1024 * 1024),
        name="proj",
    )(x2d, w["attn_g"], w["w_main"], w["q_g"], w["kv_g"], w["w_uq"], w["w_k"], w["w_vt"],
      w["w_navt"], w["cos"], w["sin_lo"], w["sin_hi"])


def _mla_kernel(q_ref, k_ref, vt_ref, o_ref, acc_ref, s_ref, *, n_q, n_sc, n_u, group):
    tq, tk = MLA_Q_TILE, TOKEN_TILE
    acc_ref[...] = jnp.zeros_like(acc_ref)

    def split(u):
        return (u // n_sc, u % n_sc) if n_sc > 1 else (u, 0)

    def produce(u, qi):
        hd, sc = split(u)
        q = q_ref[hd, qi * tq:(qi + 1) * tq, :]
        col_max = []
        for c in range(group):
            k0 = pl.multiple_of((sc * group + c) * tk, tk)
            s = lax.dot_general(k_ref[hd, pl.ds(k0, tk), :], q, _NT, preferred_element_type=F32)
            s_ref[qi % 2, c] = s
            col_max.append(jnp.max(s, axis=0, keepdims=True))
        return functools.reduce(jnp.maximum, col_max)

    def consume(u, qi, m, cmax):
        hd, sc = split(u)
        m = jnp.where(sc == 0, NEG_INF, m)
        m_new = jnp.maximum(m, cmax)
        alpha = jnp.exp2(m - m_new)
        pv = 0.0
        for c in range(group):
            p = jnp.exp2(s_ref[qi % 2, c] - m_new).astype(BF16)
            pv = pv + jnp.dot(vt_ref[hd, sc * group + c], p, preferred_element_type=F32)
        acc = alpha * acc_ref[qi] + pv
        acc_ref[qi] = acc
        row0 = hd * HEAD_DIM if isinstance(hd, int) else pl.multiple_of(hd * HEAD_DIM, HEAD_DIM)
        o_ref[pl.ds(row0, HEAD_DIM), qi * tq:(qi + 1) * tq] = (
            acc[:HEAD_DIM] / acc[HEAD_DIM:HEAD_DIM + 1])
        return m_new

    def sweep(u, ms, cmax, last):
        new_ms = []
        for qi in range(n_q):
            if qi + 1 < n_q:
                cmax_next = produce(u, qi + 1)
            else:
                cmax_next = None if last else produce(u + 1, 0)
            new_ms.append(consume(u, qi, ms[qi], cmax))
            cmax = cmax_next
        return tuple(new_ms), cmax

    ms = tuple(jnp.full((1, tq), NEG_INF, F32) for _ in range(n_q))
    carry = lax.fori_loop(0, n_u - 1, lambda u, c: sweep(u, *c, False), (ms, produce(0, 0)))
    sweep(n_u - 1, *carry, True)


def _mla_call(q, k, vt, batch, seq):
    n = q.shape[1]
    qb = min(seq, MLA_Q_BLOCK)
    n_qb = seq // qb
    n_kv = seq // TOKEN_TILE
    n_q = qb // MLA_Q_TILE
    group = math.gcd(n_kv, MLA_KV_GROUP)
    n_sc = n_kv // group
    hb = max(1, min(HEADS, MLA_MIN_SWEEPS // n_sc))
    assert n_q % 2 == 0 and HEADS % hb == 0
    kernel = functools.partial(_mla_kernel, n_q=n_q, n_sc=n_sc, n_u=hb * n_sc, group=group)
    return pl.pallas_call(
        kernel,
        grid=(batch, HEADS // hb, n_qb),
        in_specs=[
            pl.BlockSpec((hb, qb, V7X_LANES), lambda b, h, i: (h, b * n_qb + i, 0)),
            pl.BlockSpec((hb, seq, V7X_LANES), lambda b, h, i: (h, b, 0)),
            pl.BlockSpec((hb, n_kv, V_ROWS, TOKEN_TILE), lambda b, h, i: (h, b, 0, 0)),
        ],
        out_specs=pl.BlockSpec((hb * HEAD_DIM, qb), lambda b, h, i: (h, b * n_qb + i)),
        out_shape=jax.ShapeDtypeStruct((MLA_WIDTH, n), F32),
        scratch_shapes=[pltpu.VMEM((n_q, V_ROWS, MLA_Q_TILE), F32),
                        pltpu.VMEM((2, group, TOKEN_TILE, MLA_Q_TILE), F32)],
        compiler_params=pltpu.CompilerParams(
            dimension_semantics=("arbitrary", "arbitrary", "arbitrary"),
            vmem_limit_bytes=40 * 1024 * 1024),
        name="mla",
    )(q, k, vt)


def _na_kernel(q_ref, k0_ref, k1_ref, k2_ref, v0_ref, v1_ref, v2_ref, bm_ref, o_ref, s_ref):
    k_refs = (k0_ref, k1_ref, k2_ref)
    v_refs = (v0_ref, v1_ref, v2_ref)

    def produce(hd, slot):
        q = q_ref[hd]
        pair = slice((hd // 2) * V7X_LANES, (hd // 2 + 1) * V7X_LANES)
        col_max = []
        for t in range(NA_CHUNKS):
            s = (lax.dot_general(k_refs[t][:, pair], q, _NT, preferred_element_type=F32)
                 + bm_ref[0, hd, t])
            s_ref[slot, t] = s
            col_max.append(jnp.max(s, axis=0, keepdims=True))
        return functools.reduce(jnp.maximum, col_max)

    def consume(hd, slot, m):
        rows = slice(hd * HEAD_DIM, (hd + 1) * HEAD_DIM)
        l, o = 0.0, 0.0
        for t in range(NA_CHUNKS):
            p = jnp.exp2(s_ref[slot, t] - m)
            l = l + jnp.sum(p, axis=0, keepdims=True)
            o = o + jnp.dot(v_refs[t][rows, :], p.astype(BF16), preferred_element_type=F32)
        o_ref[rows, :] = o / l

    m = produce(0, 0)
    for hd in range(HEADS):
        m_next = produce(hd + 1, (hd + 1) % 2) if hd + 1 < HEADS else None
        consume(hd, hd % 2, m)
        m = m_next


def _na_call(naq, nak, navt, bias_mask, batch, seq):
    n = nak.shape[0]
    groups = seq // NA_TOK

    def win(b, g, t):
        return b * groups + jnp.clip(g - 1, 0, groups - NA_CHUNKS) + t

    def kind(g):
        return jnp.where(g == 0, 0, jnp.where(g == groups - 1, 2, 1))

    k_specs = [pl.BlockSpec((NA_TOK, NA_WIDTH), functools.partial(lambda b, g, t: (win(b, g, t), 0), t=t))
               for t in range(NA_CHUNKS)]
    v_specs = [pl.BlockSpec((NA_WIDTH, NA_TOK), functools.partial(lambda b, g, t: (0, win(b, g, t)), t=t))
               for t in range(NA_CHUNKS)]
    return pl.pallas_call(
        _na_kernel,
        grid=(batch, groups),
        in_specs=[pl.BlockSpec((HEADS, NA_TOK, V7X_LANES), lambda b, g: (0, b * groups + g, 0))]
        + k_specs + v_specs
        + [pl.BlockSpec((1, HEADS, NA_CHUNKS, NA_TOK, NA_TOK), lambda b, g: (kind(g), 0, 0, 0, 0))],
        out_specs=pl.BlockSpec((NA_WIDTH, NA_TOK), lambda b, g: (0, b * groups + g)),
        out_shape=jax.ShapeDtypeStruct((NA_WIDTH, n), F32),
        scratch_shapes=[pltpu.VMEM((2, NA_CHUNKS, NA_TOK, NA_TOK), F32)],
        compiler_params=pltpu.CompilerParams(
            dimension_semantics=("arbitrary", "arbitrary"), vmem_limit_bytes=40 * 1024 * 1024),
        name="na",
    )(naq, nak, nak, nak, navt, navt, navt, bias_mask)


def _post_kernel(at_ref, bt_ref, x_ref, ga_ref, gb_ref, wo_ref, gf_ref, wg_ref, wu_ref, wd_ref,
                 gl_ref, o_ref, acc_ref):
    mix_t = jnp.concatenate([_rms_cols(at_ref[...], ga_ref[...]),
                             _rms_cols(bt_ref[...], gb_ref[...])], axis=0).astype(BF16)
    x1 = x_ref[...] + lax.dot_general(mix_t, wo_ref[...], _TN, preferred_element_type=F32)
    h2 = _rms_rows(x1, gf_ref[...]).astype(BF16)
    acc_ref[...] = x1

    def ff_body(c, _):
        gate = jnp.dot(h2, wg_ref[c], preferred_element_type=F32)
        up = jnp.dot(h2, wu_ref[c], preferred_element_type=F32)
        act = (gate * jax.nn.sigmoid(gate) * up).astype(BF16)
        acc_ref[...] += jnp.dot(act, wd_ref[c], preferred_element_type=F32)
        return 0

    lax.fori_loop(0, D_FF // FF_CHUNK, ff_body, 0)
    o_ref[...] = _rms_rows(acc_ref[...], gl_ref[...])


def _post_call(at, bt, x2d, w):
    n = x2d.shape[0]
    tm = TOKEN_TILE
    n_ff = D_FF // FF_CHUNK
    const2 = lambda i: (0, 0)
    const3 = lambda i: (0, 0, 0)
    single = dict(pipeline_mode=pl.Buffered(1))
    return pl.pallas_call(
        _post_kernel,
        grid=(n // tm,),
        in_specs=[
            pl.BlockSpec((MLA_WIDTH, tm), lambda i: (0, i)),
            pl.BlockSpec((NA_WIDTH, tm), lambda i: (0, i)),
            pl.BlockSpec((tm, D_MODEL), lambda i: (i, 0)),
            pl.BlockSpec((MLA_WIDTH, 1), const2),
            pl.BlockSpec((NA_WIDTH, 1), const2),
            pl.BlockSpec((D_MODEL, D_MODEL), const2, **single),
            pl.BlockSpec((1, D_MODEL), const2),
            pl.BlockSpec((n_ff, D_MODEL, FF_CHUNK), const3, **single),
            pl.BlockSpec((n_ff, D_MODEL, FF_CHUNK), const3, **single),
            pl.BlockSpec((n_ff, FF_CHUNK, D_MODEL), const3, **single),
            pl.BlockSpec((1, D_MODEL), const2),
        ],
        out_specs=pl.BlockSpec((tm, D_MODEL), lambda i: (i, 0)),
        out_shape=jax.ShapeDtypeStruct((n, D_MODEL), F32),
        scratch_shapes=[pltpu.VMEM((tm, D_MODEL), F32)],
        compiler_params=pltpu.CompilerParams(
            dimension_semantics=("arbitrary",), vmem_limit_bytes=56 * 1024 * 1024),
        name="post",
    )(at, bt, x2d, w["mla_out_g"], w["na_out_g"], w["w_o"], w["ffn_g"], w["w_gate"], w["w_up"],
      w["w_down"], w["final_g"])


def _pad_heads(wm, width):
    r = wm.shape[0]
    wm = wm.reshape(r, HEADS, width)
    return jnp.pad(wm, ((0, 0), (0, 0), (0, V7X_LANES - width))).reshape(r, HEADS * V7X_LANES)


def _prep_layer(attn_norm_g, w_in, q_norm_g, kv_norm_g, w_uq, w_ukv, na_rpb, mla_out_g, na_out_g,
                w_o, ffn_norm_g, w_gate, w_up, w_down, final_norm_g):
    half = MLA_ROPE_DIM // 2
    o_kr = Q_LORA_RANK + KV_LORA_RANK
    o_na = o_kr + MLA_ROPE_DIM
    w_kr = w_in[:, o_kr:o_na]
    kr_block = jnp.zeros((D_MODEL, V7X_LANES), F32).at[:, HEAD_DIM:HEAD_DIM + MLA_ROPE_DIM].set(w_kr)
    w_nav = w_in[:, o_na + 2 * NA_WIDTH:]
    w_main = jnp.concatenate([w_in[:, :o_kr], kr_block, w_in[:, o_na:o_na + 2 * NA_WIDTH]],
                             axis=1).astype(BF16)

    w_ukv_h = w_ukv.reshape(KV_LORA_RANK, HEADS, 2 * HEAD_DIM)
    w_k = _pad_heads(w_ukv_h[:, :, :HEAD_DIM].reshape(KV_LORA_RANK, MLA_WIDTH), HEAD_DIM)
    w_v = w_ukv_h[:, :, HEAD_DIM:].reshape(KV_LORA_RANK, MLA_WIDTH)
    n_ff = D_FF // FF_CHUNK
    del half
    return {
        "attn_g": attn_norm_g.reshape(1, D_MODEL),
        "w_main": w_main,
        "q_g": q_norm_g.reshape(1, Q_LORA_RANK),
        "kv_g": kv_norm_g.reshape(1, KV_LORA_RANK),
        "w_uq": _pad_heads(w_uq, MLA_QK_DIM).astype(BF16),
        "w_k": w_k.astype(BF16),
        "w_vt": w_v.T.astype(BF16),
        "w_navt": w_nav.T.astype(BF16),
        "rpb": na_rpb,
        "mla_out_g": mla_out_g.reshape(MLA_WIDTH, 1),
        "na_out_g": na_out_g.reshape(NA_WIDTH, 1),
        "w_o": w_o.astype(BF16),
        "ffn_g": ffn_norm_g.reshape(1, D_MODEL),
        "w_gate": w_gate.reshape(D_MODEL, n_ff, FF_CHUNK).transpose(1, 0, 2).astype(BF16),
        "w_up": w_up.reshape(D_MODEL, n_ff, FF_CHUNK).transpose(1, 0, 2).astype(BF16),
        "w_down": w_down.reshape(n_ff, FF_CHUNK, D_MODEL).astype(BF16),
        "final_g": final_norm_g.reshape(1, D_MODEL),
    }


def _rope_tables(seq):
    half = MLA_ROPE_DIM // 2
    inv = ROPE_THETA ** (-jnp.arange(0, MLA_ROPE_DIM, 2, dtype=F32) / MLA_ROPE_DIM)
    ang = jnp.arange(seq, dtype=F32)[:, None] * inv[None, :]
    cos, sin = jnp.cos(ang), jnp.sin(ang)
    zeros = jnp.zeros((seq, half), F32)
    ones = jnp.ones((seq, HEAD_DIM), F32)
    tail = jnp.zeros((seq, V7X_LANES - MLA_QK_DIM), F32)
    cos_t = jnp.concatenate([ones, cos, cos, tail], axis=1)
    lead = jnp.zeros((seq, HEAD_DIM), F32)
    sin_lo = jnp.concatenate([lead, -sin, zeros, tail], axis=1)
    sin_hi = jnp.concatenate([lead, zeros, sin, tail], axis=1)
    return cos_t, sin_lo, sin_hi


def _na_bias_mask(rpb):
    qc = np.arange(GRID_W)[None, :]
    kc = np.arange(GRID_W)[:, None]
    col_start = np.clip(qc - NA_KW // 2, 0, GRID_W - NA_KW)
    col_ok = (kc >= col_start) & (kc < col_start + NA_KW)
    period = 2 * GRID_W
    n_d = 2 * NA_KH - 1
    row = jnp.concatenate([rpb[..., NA_KW - 1::-1],
                           jnp.zeros((HEADS, n_d, period - (2 * NA_KW - 1)), F32),
                           rpb[..., :NA_KW - 1:-1]], axis=-1)
    skew = jnp.broadcast_to(row[:, :, None, :], (HEADS, n_d, GRID_W, period))
    skew = skew.reshape(HEADS, n_d, GRID_W * period)[..., :GRID_W * (period - 1)]
    toe = skew.reshape(HEADS, n_d, GRID_W, period - 1)[..., :GRID_W]

    i = np.arange(NA_CHUNKS * NA_ROWS)[:, None]
    j = np.arange(NA_ROWS)[None, :]
    kinds = [(0, (i < NA_KH) & (j >= 0)),
             (-NA_ROWS, (i - j >= 0) & (i - j < NA_KH)),
             (-2 * NA_ROWS, (i >= NA_ROWS) & (j >= 0))]
    di = np.stack([np.clip(off + i - j + NA_KH - 1, 0, n_d - 1) for off, _ in kinds])
    ok = np.stack([r[:, None, :, None] & col_ok[None, :, None, :] for _, r in kinds])
    bias = toe[:, di]
    bias = bias.transpose(1, 0, 2, 4, 3, 5)
    bm = jnp.where(ok[:, None], bias * LOG2_E, NEG_INF)
    return bm.reshape(3, HEADS, NA_CHUNKS, NA_TOK, NA_TOK)


def _run_trunk(x, w, bias_mask):
    batch, seq, _ = x.shape
    assert seq % TOKEN_TILE == 0 and seq % NA_TOK == 0 and seq // NA_TOK >= NA_CHUNKS
    assert seq % min(seq, MLA_Q_BLOCK) == 0
    x2d = x.reshape(batch * seq, D_MODEL)
    cos_t, sin_lo, sin_hi = _rope_tables(seq)
    wt = dict(w, cos=cos_t, sin_lo=sin_lo, sin_hi=sin_hi)
    q, k, vt, naq, nak, navt = _proj_call(x2d, seq, wt)
    at = _mla_call(q, k, vt, batch, seq)
    bt = _na_call(naq, nak, navt, bias_mask, batch, seq)
    y = _post_call(at, bt, x2d, w)
    return y.reshape(batch, seq, D_MODEL)


def kernel(x_prompt, x_sample, attn_norm_g, w_in, q_norm_g, kv_norm_g, w_uq, w_ukv, na_rpb,
           mla_out_g, na_out_g, w_o, ffn_norm_g, w_gate, w_up, w_down, final_norm_g):
    assert attn_norm_g.shape[0] == 1, "single-layer trunk"
    w = _prep_layer(attn_norm_g[0], w_in[0], q_norm_g[0], kv_norm_g[0], w_uq[0], w_ukv[0], na_rpb[0],
                    mla_out_g[0], na_out_g[0], w_o[0], ffn_norm_g[0], w_gate[0], w_up[0], w_down[0],
                    final_norm_g)
    bias_mask = _na_bias_mask(w["rpb"])
    return (_run_trunk(x_prompt, w, bias_mask), _run_trunk(x_sample, w, bias_mask))
```

```python
import functools
import math

import jax
import jax.numpy as jnp
import numpy as np
from jax import lax
from jax.experimental import pallas as pl
from jax.experimental.pallas import tpu as pltpu

D_MODEL = 1024
MLA_WIDTH = D_MODEL // 2
NA_WIDTH = D_MODEL - MLA_WIDTH
HEADS = 8
HEAD_DIM = 64
MLA_ROPE_DIM = 32
MLA_QK_DIM = HEAD_DIM + MLA_ROPE_DIM
Q_LORA_RANK = 384
KV_LORA_RANK = 256
ROPE_THETA = 10000.0
GRID_W = 64
NA_KH = 8
NA_KW = 16
D_FF = int(math.ceil(8 * D_MODEL / 3 / 256)) * 256
EPS = 1e-6
NEG_INF = -1e30
LOG2_E = math.log2(math.e)
V_ROWS = HEAD_DIM + 16

V7X_LANES = 128
V7X_VMEM_BYTES = 64 * 1024 * 1024

TOKEN_TILE = 512
MLA_Q_TILE = 256
MLA_Q_BLOCK = 2048
MLA_KV_GROUP = 4
MLA_MIN_SWEEPS = 4
NA_ROWS = 4
NA_TOK = NA_ROWS * GRID_W
NA_CHUNKS = 3
FF_CHUNK = 256

_C_Q = 0
_C_KV = _C_Q + Q_LORA_RANK
_C_KR = _C_KV + KV_LORA_RANK
_C_NAQ = _C_KR + V7X_LANES
_C_NAK = _C_NAQ + NA_WIDTH
_C_END = _C_NAK + NA_WIDTH

_NT = (((1,), (1,)), ((), ()))
_TN = (((0,), (0,)), ((), ()))
BF16 = jnp.bfloat16
F32 = jnp.float32


def _rms_rows(x, g):
    return x * lax.rsqrt(jnp.mean(x * x, axis=-1, keepdims=True) + EPS) * g


def _rms_cols(xt, g):
    return xt * lax.rsqrt(jnp.mean(xt * xt, axis=0, keepdims=True) + EPS) * g


def _rope_lanes(x, cos, sin_lo, sin_hi):
    half = MLA_ROPE_DIM // 2
    return (x * cos + pltpu.roll(x, V7X_LANES - half, 1) * sin_lo
            + pltpu.roll(x, half, 1) * sin_hi)


def _rows(start, size):
    return pl.ds(start if isinstance(start, int) else pl.multiple_of(start, size), size)


def _proj_kernel(x_ref, ga_ref, wmain_ref, gq_ref, gkv_ref, wuq_ref, wk_ref, wvt_ref, wnavt_ref,
                 cos_ref, sinlo_ref, sinhi_ref,
                 q_ref, k_ref, vt_ref, naq_ref, nak_ref, navt_ref):
    h = _rms_rows(x_ref[...], ga_ref[...]).astype(BF16)

    def proj(lo, hi):
        return jnp.dot(h, wmain_ref[:, lo:hi], preferred_element_type=F32)

    cqn = _rms_rows(proj(_C_Q, _C_KV), gq_ref[...]).astype(BF16)
    ckvn = _rms_rows(proj(_C_KV, _C_KR), gkv_ref[...]).astype(BF16)
    cos, sin_lo, sin_hi = cos_ref[...], sinlo_ref[...], sinhi_ref[...]
    k_rope = _rope_lanes(proj(_C_KR, _C_NAQ), cos, sin_lo, sin_hi)

    q = jnp.dot(cqn, wuq_ref[...], preferred_element_type=F32)
    kn = jnp.dot(ckvn, wk_ref[...], preferred_element_type=F32)
    vt = lax.dot_general(wvt_ref[...], ckvn, _NT, preferred_element_type=F32)
    naq = proj(_C_NAQ, _C_NAK) * (HEAD_DIM ** -0.5 * LOG2_E)
    scale = MLA_QK_DIM ** -0.5 * LOG2_E
    ones = jnp.ones((V_ROWS - HEAD_DIM, vt.shape[1]), BF16)
    lane_half = lax.broadcasted_iota(jnp.int32, (1, V7X_LANES), 1) // HEAD_DIM
    for hd in range(HEADS):
        lanes = slice(hd * V7X_LANES, (hd + 1) * V7X_LANES)
        q_ref[hd] = (_rope_lanes(q[:, lanes], cos, sin_lo, sin_hi) * scale).astype(BF16)
        k_ref[hd] = (kn[:, lanes] + k_rope).astype(BF16)
        vt_ref[hd, 0, :HEAD_DIM, :] = vt[hd * HEAD_DIM:(hd + 1) * HEAD_DIM, :].astype(BF16)
        vt_ref[hd, 0, HEAD_DIM:, :] = ones
        pair = naq[:, (hd // 2) * V7X_LANES:(hd // 2 + 1) * V7X_LANES]
        naq_ref[hd] = jnp.where(lane_half == hd % 2, pair, 0.0).astype(BF16)
    nak = proj(_C_NAK, _C_END).astype(BF16)
    for pr in range(HEADS // 2):
        nak_ref[pr] = nak[:, pr * V7X_LANES:(pr + 1) * V7X_LANES]
    navt = lax.dot_general(wnavt_ref[...], h, _NT, preferred_element_type=F32).astype(BF16)
    for hd in range(HEADS):
        navt_ref[hd * V_ROWS:hd * V_ROWS + HEAD_DIM, :] = navt[hd * HEAD_DIM:(hd + 1) * HEAD_DIM, :]
        navt_ref[hd * V_ROWS + HEAD_DIM:(hd + 1) * V_ROWS, :] = ones


def _proj_call(x2d, seq, w):
    n = x2d.shape[0]
    tm = TOKEN_TILE
    n_tiles = n // tm
    pos_tiles = seq // tm
    const = lambda i: (0, 0)
    single = dict(pipeline_mode=pl.Buffered(1))
    in_specs = [
        pl.BlockSpec((tm, D_MODEL), lambda i: (i, 0)),
        pl.BlockSpec((1, D_MODEL), const),
        pl.BlockSpec((D_MODEL, _C_END), const, **single),
        pl.BlockSpec((1, Q_LORA_RANK), const),
        pl.BlockSpec((1, KV_LORA_RANK), const),
        pl.BlockSpec((Q_LORA_RANK, HEADS * V7X_LANES), const, **single),
        pl.BlockSpec((KV_LORA_RANK, HEADS * V7X_LANES), const, **single),
        pl.BlockSpec((MLA_WIDTH, KV_LORA_RANK), const, **single),
        pl.BlockSpec((NA_WIDTH, D_MODEL), const, **single),
        pl.BlockSpec((tm, V7X_LANES), lambda i: (i % pos_tiles, 0)),
        pl.BlockSpec((tm, V7X_LANES), lambda i: (i % pos_tiles, 0)),
        pl.BlockSpec((tm, V7X_LANES), lambda i: (i % pos_tiles, 0)),
    ]
    out_shape = (
        jax.ShapeDtypeStruct((HEADS, n, V7X_LANES), BF16),
        jax.ShapeDtypeStruct((HEADS, n, V7X_LANES), BF16),
        jax.ShapeDtypeStruct((HEADS, n_tiles, V_ROWS, tm), BF16),
        jax.ShapeDtypeStruct((HEADS, n, V7X_LANES), BF16),
        jax.ShapeDtypeStruct((HEADS // 2, n, V7X_LANES), BF16),
        jax.ShapeDtypeStruct((HEADS * V_ROWS, n), BF16),
    )
    out_specs = (
        pl.BlockSpec((HEADS, tm, V7X_LANES), lambda i: (0, i, 0)),
        pl.BlockSpec((HEADS, tm, V7X_LANES), lambda i: (0, i, 0)),
        pl.BlockSpec((HEADS, 1, V_ROWS, tm), lambda i: (0, i, 0, 0)),
        pl.BlockSpec((HEADS, tm, V7X_LANES), lambda i: (0, i, 0)),
        pl.BlockSpec((HEADS // 2, tm, V7X_LANES), lambda i: (0, i, 0)),
        pl.BlockSpec((HEADS * V_ROWS, tm), lambda i: (0, i)),
    )
    return pl.pallas_call(
        _proj_kernel,
        grid=(n_tiles,),
        in_specs=in_specs,
        out_specs=out_specs,
        out_shape=out_shape,
        compiler_params=pltpu.CompilerParams(
            dimension_semantics=("arbitrary",), vmem_limit_bytes=48 * 1024 * 1024),
        name="proj",
    )(x2d, w["attn_g"], w["w_main"], w["q_g"], w["kv_g"], w["w_uq"], w["w_k"], w["w_vt"],
      w["w_navt"], w["cos"], w["sin_lo"], w["sin_hi"])


def _mla_kernel(q_ref, k_ref, vt_ref, o_ref, acc_ref, s_ref, *, n_q, n_sc, n_u, group):
    tq, tk = MLA_Q_TILE, TOKEN_TILE
    acc_ref[...] = jnp.zeros_like(acc_ref)

    def split(u):
        return (u // n_sc, u % n_sc) if n_sc > 1 else (u, 0)

    def produce(u, qi):
        hd, sc = split(u)
        q = q_ref[hd, qi * tq:(qi + 1) * tq, :]
        col_max = []
        for c in range(group):
            k0 = pl.multiple_of((sc * group + c) * tk, tk)
            s = lax.dot_general(k_ref[hd, pl.ds(k0, tk), :], q, _NT, preferred_element_type=F32)
            s_ref[qi % 2, c] = s
            col_max.append(jnp.max(s, axis=0, keepdims=True))
        return functools.reduce(jnp.maximum, col_max)

    def consume(u, qi, m, cmax):
        hd, sc = split(u)
        m = jnp.where(sc == 0, NEG_INF, m)
        m_new = jnp.maximum(m, cmax)
        alpha = jnp.exp2(m - m_new)
        pv = 0.0
        for c in range(group):
            p = jnp.exp2(s_ref[qi % 2, c] - m_new).astype(BF16)
            pv = pv + jnp.dot(vt_ref[hd, sc * group + c], p, preferred_element_type=F32)
        acc = alpha * acc_ref[qi] + pv
        acc_ref[qi] = acc
        o_ref[_rows(hd * HEAD_DIM, HEAD_DIM), qi * tq:(qi + 1) * tq] = (
            acc[:HEAD_DIM] / acc[HEAD_DIM:HEAD_DIM + 1])
        return m_new

    def sweep(u, ms, cmax, last):
        new_ms = []
        for qi in range(n_q):
            if qi + 1 < n_q:
                cmax_next = produce(u, qi + 1)
            else:
                cmax_next = None if last else produce(u + 1, 0)
            new_ms.append(consume(u, qi, ms[qi], cmax))
            cmax = cmax_next
        return tuple(new_ms), cmax

    ms = tuple(jnp.full((1, tq), NEG_INF, F32) for _ in range(n_q))
    carry = lax.fori_loop(0, n_u - 1, lambda u, c: sweep(u, *c, False), (ms, produce(0, 0)))
    sweep(n_u - 1, *carry, True)


def _mla_call(q, k, vt, batch, seq):
    n = q.shape[1]
    qb = min(seq, MLA_Q_BLOCK)
    n_qb = seq // qb
    n_kv = seq // TOKEN_TILE
    n_q = qb // MLA_Q_TILE
    group = math.gcd(n_kv, MLA_KV_GROUP)
    n_sc = n_kv // group
    hb = max(1, min(HEADS, MLA_MIN_SWEEPS // n_sc))
    assert n_q % 2 == 0 and HEADS % hb == 0
    kernel = functools.partial(_mla_kernel, n_q=n_q, n_sc=n_sc, n_u=hb * n_sc, group=group)
    return pl.pallas_call(
        kernel,
        grid=(batch, HEADS // hb, n_qb),
        in_specs=[
            pl.BlockSpec((hb, qb, V7X_LANES), lambda b, h, i: (h, b * n_qb + i, 0)),
            pl.BlockSpec((hb, seq, V7X_LANES), lambda b, h, i: (h, b, 0)),
            pl.BlockSpec((hb, n_kv, V_ROWS, TOKEN_TILE), lambda b, h, i: (h, b, 0, 0)),
        ],
        out_specs=pl.BlockSpec((hb * HEAD_DIM, qb), lambda b, h, i: (h, b * n_qb + i)),
        out_shape=jax.ShapeDtypeStruct((MLA_WIDTH, n), F32),
        scratch_shapes=[pltpu.VMEM((n_q, V_ROWS, MLA_Q_TILE), F32),
                        pltpu.VMEM((2, group, TOKEN_TILE, MLA_Q_TILE), F32)],
        compiler_params=pltpu.CompilerParams(
            dimension_semantics=("arbitrary", "arbitrary", "arbitrary"),
            vmem_limit_bytes=40 * 1024 * 1024),
        name="mla",
    )(q, k, vt)


def _na_kernel(q_ref, k0_ref, k1_ref, k2_ref, k3_ref, v0_ref, v1_ref, v2_ref, v3_ref, bm_ref, o_ref,
               s_ref, *, groups):
    k_refs = (k0_ref, k1_ref, k2_ref, k3_ref)
    v_refs = (v0_ref, v1_ref, v2_ref, v3_ref)
    first_group = 2 * pl.program_id(1)
    kinds = [jnp.where(first_group + e == 0, 0, jnp.where(first_group + e == groups - 1, 2, 1))
             for e in range(2)]

    def produce(hd, e):
        q = q_ref[hd, e * NA_TOK:(e + 1) * NA_TOK, :]
        col_max = []
        for t in range(NA_CHUNKS):
            s = (lax.dot_general(k_refs[e + t][hd // 2], q, _NT, preferred_element_type=F32)
                 + bm_ref[kinds[e], hd, t])
            s_ref[e, t] = s
            col_max.append(jnp.max(s, axis=0, keepdims=True))
        return functools.reduce(jnp.maximum, col_max)

    def consume(hd, e, m):
        v_rows = _rows(hd * V_ROWS, V_ROWS)
        o = 0.0
        for t in range(NA_CHUNKS):
            p = jnp.exp2(s_ref[e, t] - m).astype(BF16)
            o = o + jnp.dot(v_refs[e + t][v_rows, :], p, preferred_element_type=F32)
        o_ref[_rows(hd * HEAD_DIM, HEAD_DIM), e * NA_TOK:(e + 1) * NA_TOK] = (
            o[:HEAD_DIM] / o[HEAD_DIM:HEAD_DIM + 1])

    def head_pair(hd, m, last):
        items = [(hd, 0), (hd, 1), (hd + 1, 0), (hd + 1, 1)]
        for n, item in enumerate(items):
            if n + 1 < len(items):
                m_next = produce(*items[n + 1])
            else:
                m_next = None if last else produce(hd + 2, 0)
            consume(*item, m)
            m = m_next
        return m

    m = lax.fori_loop(0, HEADS // 2 - 1, lambda i, m: head_pair(2 * i, m, False), produce(0, 0))
    head_pair(HEADS - 2, m, True)


def _na_call(naq, nak, navt, bias_mask, batch, seq):
    n = nak.shape[1]
    groups = seq // NA_TOK
    steps = groups // 2
    assert groups % 2 == 0 and groups >= NA_CHUNKS + 1

    def chunk(b, i, t):
        return b * groups + jnp.clip(2 * i - 1 + t, 0, groups - 1)

    n_win = NA_CHUNKS + 1
    k_specs = [pl.BlockSpec((HEADS // 2, NA_TOK, V7X_LANES),
                            functools.partial(lambda b, i, t: (0, chunk(b, i, t), 0), t=t))
               for t in range(n_win)]
    v_specs = [pl.BlockSpec((HEADS * V_ROWS, NA_TOK),
                            functools.partial(lambda b, i, t: (0, chunk(b, i, t)), t=t))
               for t in range(n_win)]
    return pl.pallas_call(
        functools.partial(_na_kernel, groups=groups),
        grid=(batch, steps),
        in_specs=[pl.BlockSpec((HEADS, 2 * NA_TOK, V7X_LANES), lambda b, i: (0, b * steps + i, 0))]
        + k_specs + v_specs
        + [pl.BlockSpec((3, HEADS, NA_CHUNKS, NA_TOK, NA_TOK), lambda b, i: (0, 0, 0, 0, 0),
                        pipeline_mode=pl.Buffered(1))],
        out_specs=pl.BlockSpec((NA_WIDTH, 2 * NA_TOK), lambda b, i: (0, b * steps + i)),
        out_shape=jax.ShapeDtypeStruct((NA_WIDTH, n), F32),
        scratch_shapes=[pltpu.VMEM((2, NA_CHUNKS, NA_TOK, NA_TOK), F32)],
        compiler_params=pltpu.CompilerParams(
            dimension_semantics=("arbitrary", "arbitrary"), vmem_limit_bytes=48 * 1024 * 1024),
        name="na",
    )(naq, *([nak] * n_win), *([navt] * n_win), bias_mask)


def _post_kernel(at_ref, bt_ref, x_ref, ga_ref, gb_ref, wo_ref, gf_ref, wg_ref, wu_ref, wd_ref,
                 gl_ref, o_ref, acc_ref, gu_ref):
    mix_t = jnp.concatenate([_rms_cols(at_ref[...], ga_ref[...]),
                             _rms_cols(bt_ref[...], gb_ref[...])], axis=0).astype(BF16)
    x1 = x_ref[...] + lax.dot_general(mix_t, wo_ref[...], _TN, preferred_element_type=F32)
    h2 = _rms_rows(x1, gf_ref[...]).astype(BF16)
    acc_ref[...] = x1

    def produce(c, slot):
        gu_ref[slot, 0] = jnp.dot(h2, wg_ref[c], preferred_element_type=F32)
        gu_ref[slot, 1] = jnp.dot(h2, wu_ref[c], preferred_element_type=F32)

    def consume(c, slot):
        gate = gu_ref[slot, 0]
        act = (gate * jax.nn.sigmoid(gate) * gu_ref[slot, 1]).astype(BF16)
        acc_ref[...] += jnp.dot(act, wd_ref[c], preferred_element_type=F32)

    def ff_pair(jj, _):
        c = 2 * jj
        produce(c + 1, 1)
        consume(c, 0)
        produce(c + 2, 0)
        consume(c + 1, 1)
        return 0

    n_ff = D_FF // FF_CHUNK
    assert n_ff % 2 == 1
    produce(0, 0)
    lax.fori_loop(0, n_ff // 2, ff_pair, 0)
    consume(n_ff - 1, 0)
    o_ref[...] = _rms_rows(acc_ref[...], gl_ref[...])


def _post_call(at, bt, x2d, w):
    n = x2d.shape[0]
    tm = TOKEN_TILE
    n_ff = D_FF // FF_CHUNK
    const2 = lambda i: (0, 0)
    const3 = lambda i: (0, 0, 0)
    single = dict(pipeline_mode=pl.Buffered(1))
    return pl.pallas_call(
        _post_kernel,
        grid=(n // tm,),
        in_specs=[
            pl.BlockSpec((MLA_WIDTH, tm), lambda i: (0, i)),
            pl.BlockSpec((NA_WIDTH, tm), lambda i: (0, i)),
            pl.BlockSpec((tm, D_MODEL), lambda i: (i, 0)),
            pl.BlockSpec((MLA_WIDTH, 1), const2),
            pl.BlockSpec((NA_WIDTH, 1), const2),
            pl.BlockSpec((D_MODEL, D_MODEL), const2, **single),
            pl.BlockSpec((1, D_MODEL), const2),
            pl.BlockSpec((n_ff, D_MODEL, FF_CHUNK), const3, **single),
            pl.BlockSpec((n_ff, D_MODEL, FF_CHUNK), const3, **single),
            pl.BlockSpec((n_ff, FF_CHUNK, D_MODEL), const3, **single),
            pl.BlockSpec((1, D_MODEL), const2),
        ],
        out_specs=pl.BlockSpec((tm, D_MODEL), lambda i: (i, 0)),
        out_shape=jax.ShapeDtypeStruct((n, D_MODEL), F32),
        scratch_shapes=[pltpu.VMEM((tm, D_MODEL), F32), pltpu.VMEM((2, 2, tm, FF_CHUNK), F32)],
        compiler_params=pltpu.CompilerParams(
            dimension_semantics=("arbitrary",), vmem_limit_bytes=56 * 1024 * 1024),
        name="post",
    )(at, bt, x2d, w["mla_out_g"], w["na_out_g"], w["w_o"], w["ffn_g"], w["w_gate"], w["w_up"],
      w["w_down"], w["final_g"])


def _pad_heads(wm, width):
    r = wm.shape[0]
    wm = wm.reshape(r, HEADS, width)
    return jnp.pad(wm, ((0, 0), (0, 0), (0, V7X_LANES - width))).reshape(r, HEADS * V7X_LANES)


def _prep_layer(attn_norm_g, w_in, q_norm_g, kv_norm_g, w_uq, w_ukv, na_rpb, mla_out_g, na_out_g,
                w_o, ffn_norm_g, w_gate, w_up, w_down, final_norm_g):
    o_kr = Q_LORA_RANK + KV_LORA_RANK
    o_na = o_kr + MLA_ROPE_DIM
    w_kr = w_in[:, o_kr:o_na]
    kr_block = jnp.zeros((D_MODEL, V7X_LANES), F32).at[:, HEAD_DIM:HEAD_DIM + MLA_ROPE_DIM].set(w_kr)
    w_nav = w_in[:, o_na + 2 * NA_WIDTH:]
    w_main = jnp.concatenate([w_in[:, :o_kr], kr_block, w_in[:, o_na:o_na + 2 * NA_WIDTH]],
                             axis=1).astype(BF16)

    w_ukv_h = w_ukv.reshape(KV_LORA_RANK, HEADS, 2 * HEAD_DIM)
    w_k = _pad_heads(w_ukv_h[:, :, :HEAD_DIM].reshape(KV_LORA_RANK, MLA_WIDTH), HEAD_DIM)
    w_v = w_ukv_h[:, :, HEAD_DIM:].reshape(KV_LORA_RANK, MLA_WIDTH)
    n_ff = D_FF // FF_CHUNK
    return {
        "attn_g": attn_norm_g.reshape(1, D_MODEL),
        "w_main": w_main,
        "q_g": q_norm_g.reshape(1, Q_LORA_RANK),
        "kv_g": kv_norm_g.reshape(1, KV_LORA_RANK),
        "w_uq": _pad_heads(w_uq, MLA_QK_DIM).astype(BF16),
        "w_k": w_k.astype(BF16),
        "w_vt": w_v.T.astype(BF16),
        "w_navt": w_nav.T.astype(BF16),
        "rpb": na_rpb,
        "mla_out_g": mla_out_g.reshape(MLA_WIDTH, 1),
        "na_out_g": na_out_g.reshape(NA_WIDTH, 1),
        "w_o": w_o.astype(BF16),
        "ffn_g": ffn_norm_g.reshape(1, D_MODEL),
        "w_gate": w_gate.reshape(D_MODEL, n_ff, FF_CHUNK).transpose(1, 0, 2).astype(BF16),
        "w_up": w_up.reshape(D_MODEL, n_ff, FF_CHUNK).transpose(1, 0, 2).astype(BF16),
        "w_down": w_down.reshape(n_ff, FF_CHUNK, D_MODEL).astype(BF16),
        "final_g": final_norm_g.reshape(1, D_MODEL),
    }


def _rope_tables(seq):
    half = MLA_ROPE_DIM // 2
    inv = ROPE_THETA ** (-jnp.arange(0, MLA_ROPE_DIM, 2, dtype=F32) / MLA_ROPE_DIM)
    ang = jnp.arange(seq, dtype=F32)[:, None] * inv[None, :]
    cos, sin = jnp.cos(ang), jnp.sin(ang)
    zeros = jnp.zeros((seq, half), F32)
    ones = jnp.ones((seq, HEAD_DIM), F32)
    tail = jnp.zeros((seq, V7X_LANES - MLA_QK_DIM), F32)
    cos_t = jnp.concatenate([ones, cos, cos, tail], axis=1)
    lead = jnp.zeros((seq, HEAD_DIM), F32)
    sin_lo = jnp.concatenate([lead, -sin, zeros, tail], axis=1)
    sin_hi = jnp.concatenate([lead, zeros, sin, tail], axis=1)
    return cos_t, sin_lo, sin_hi


def _na_bias_mask(rpb):
    qc = np.arange(GRID_W)[None, :]
    kc = np.arange(GRID_W)[:, None]
    col_start = np.clip(qc - NA_KW // 2, 0, GRID_W - NA_KW)
    col_ok = (kc >= col_start) & (kc < col_start + NA_KW)
    period = 2 * GRID_W
    n_d = 2 * NA_KH - 1
    rpb = rpb * LOG2_E
    row = jnp.concatenate([rpb[..., NA_KW - 1::-1],
                           jnp.zeros((HEADS, n_d, period - (2 * NA_KW - 1)), F32),
                           rpb[..., :NA_KW - 1:-1]], axis=-1)
    skew = jnp.broadcast_to(row[:, :, None, :], (HEADS, n_d, GRID_W, period))
    skew = skew.reshape(HEADS, n_d, GRID_W * period)[..., :GRID_W * (period - 1)]
    toe = skew.reshape(HEADS, n_d, GRID_W, period - 1)[..., :GRID_W]

    i = np.arange(NA_CHUNKS * NA_ROWS)[:, None]
    j = np.arange(NA_ROWS)[None, :]
    di = np.clip(i - NA_ROWS - j + NA_KH - 1, 0, n_d - 1)
    row_ok = np.stack([(i >= NA_ROWS) & (j >= 0),
                       (i - j >= 0) & (i - j < NA_KH),
                       (i < NA_KH) & (j >= 0)])
    ok = row_ok[:, :, None, :, None] & col_ok[None, None, :, None, :]
    bias = toe[:, di].transpose(0, 1, 3, 2, 4)
    bm = jnp.where(ok[:, None], bias[None], NEG_INF)
    return bm.reshape(3, HEADS, NA_CHUNKS, NA_TOK, NA_TOK)


def _run_trunk(x, w, bias_mask):
    batch, seq, _ = x.shape
    assert seq % TOKEN_TILE == 0 and seq % (2 * NA_TOK) == 0
    assert seq % min(seq, MLA_Q_BLOCK) == 0
    x2d = x.reshape(batch * seq, D_MODEL)
    cos_t, sin_lo, sin_hi = _rope_tables(seq)
    wt = dict(w, cos=cos_t, sin_lo=sin_lo, sin_hi=sin_hi)
    q, k, vt, naq, nak, navt = _proj_call(x2d, seq, wt)
    at = _mla_call(q, k, vt, batch, seq)
    bt = _na_call(naq, nak, navt, bias_mask, batch, seq)
    y = _post_call(at, bt, x2d, w)
    return y.reshape(batch, seq, D_MODEL)


def kernel(x_prompt, x_sample, attn_norm_g, w_in, q_norm_g, kv_norm_g, w_uq, w_ukv, na_rpb,
           mla_out_g, na_out_g, w_o, ffn_norm_g, w_gate, w_up, w_down, final_norm_g):
    assert attn_norm_g.shape[0] == 1, "single-layer trunk"
    w = _prep_layer(attn_norm_g[0], w_in[0], q_norm_g[0], kv_norm_g[0], w_uq[0], w_ukv[0], na_rpb[0],
                    mla_out_g[0], na_out_g[0], w_o[0], ffn_norm_g[0], w_gate[0], w_up[0], w_down[0],
                    final_norm_g)
    bias_mask = _na_bias_mask(w["rpb"])
    return (_run_trunk(x_prompt, w, bias_mask), _run_trunk(x_sample, w, bias_mask))
```

```python
import functools
import math

import jax
import jax.numpy as jnp
import numpy as np
from jax import lax
from jax.experimental import pallas as pl
from jax.experimental.pallas import tpu as pltpu

D_MODEL = 1024
MLA_WIDTH = D_MODEL // 2
NA_WIDTH = D_MODEL - MLA_WIDTH
HEADS = 8
HEAD_DIM = 64
MLA_ROPE_DIM = 32
MLA_QK_DIM = HEAD_DIM + MLA_ROPE_DIM
Q_LORA_RANK = 384
KV_LORA_RANK = 256
ROPE_THETA = 10000.0
GRID_W = 64
NA_KH = 8
NA_KW = 16
D_FF = int(math.ceil(8 * D_MODEL / 3 / 256)) * 256
EPS = 1e-6
NEG_INF = -1e30
LOG2_E = math.log2(math.e)
V_ROWS = HEAD_DIM + 16

V7X_LANES = 128
V7X_VMEM_BYTES = 64 * 1024 * 1024

TOKEN_TILE = 512
MLA_Q_TILE = 256
MLA_Q_BLOCK = 2048
MLA_KV_GROUP = 4
MLA_MIN_SWEEPS = 8
NA_ROWS = 4
NA_TOK = NA_ROWS * GRID_W
NA_CHUNKS = 3
NA_GROUPS = 4
FF_CHUNK = 256

_C_Q = 0
_C_KV = _C_Q + Q_LORA_RANK
_C_KR = _C_KV + KV_LORA_RANK
_C_NAQ = _C_KR + V7X_LANES
_C_NAK = _C_NAQ + NA_WIDTH
_C_END = _C_NAK + NA_WIDTH

_NT = (((1,), (1,)), ((), ()))
_TN = (((0,), (0,)), ((), ()))
BF16 = jnp.bfloat16
F32 = jnp.float32


def _rms_rows(x, g):
    return x * lax.rsqrt(jnp.mean(x * x, axis=-1, keepdims=True) + EPS) * g


def _rms_cols(xt, g):
    return xt * lax.rsqrt(jnp.mean(xt * xt, axis=0, keepdims=True) + EPS) * g


def _rope_lanes(x, cos, sin_lo, sin_hi):
    half = MLA_ROPE_DIM // 2
    return (x * cos + pltpu.roll(x, V7X_LANES - half, 1) * sin_lo
            + pltpu.roll(x, half, 1) * sin_hi)


def _rows(start, size):
    return pl.ds(start if isinstance(start, int) else pl.multiple_of(start, size), size)


def _proj_kernel(x_ref, ga_ref, wmain_ref, gq_ref, gkv_ref, wuq_ref, wk_ref, wvt_ref, wnavt_ref,
                 cos_ref, sinlo_ref, sinhi_ref,
                 q_ref, k_ref, vt_ref, naq_ref, nak_ref, navt_ref):
    h = _rms_rows(x_ref[...], ga_ref[...]).astype(BF16)

    def proj(lo, hi):
        return jnp.dot(h, wmain_ref[:, lo:hi], preferred_element_type=F32)

    cqn = _rms_rows(proj(_C_Q, _C_KV), gq_ref[...]).astype(BF16)
    ckvn = _rms_rows(proj(_C_KV, _C_KR), gkv_ref[...]).astype(BF16)
    cos, sin_lo, sin_hi = cos_ref[...], sinlo_ref[...], sinhi_ref[...]
    k_rope = _rope_lanes(proj(_C_KR, _C_NAQ), cos, sin_lo, sin_hi)

    q = jnp.dot(cqn, wuq_ref[...], preferred_element_type=F32)
    kn = jnp.dot(ckvn, wk_ref[...], preferred_element_type=F32)
    vt = lax.dot_general(wvt_ref[...], ckvn, _NT, preferred_element_type=F32)
    naq = proj(_C_NAQ, _C_NAK) * (HEAD_DIM ** -0.5 * LOG2_E)
    scale = MLA_QK_DIM ** -0.5 * LOG2_E
    ones = jnp.ones((V_ROWS - HEAD_DIM, vt.shape[1]), BF16)
    lane_half = lax.broadcasted_iota(jnp.int32, (1, V7X_LANES), 1) // HEAD_DIM
    for hd in range(HEADS):
        lanes = slice(hd * V7X_LANES, (hd + 1) * V7X_LANES)
        q_ref[hd] = (_rope_lanes(q[:, lanes], cos, sin_lo, sin_hi) * scale).astype(BF16)
        k_ref[hd] = (kn[:, lanes] + k_rope).astype(BF16)
        vt_ref[hd, 0, :HEAD_DIM, :] = vt[hd * HEAD_DIM:(hd + 1) * HEAD_DIM, :].astype(BF16)
        vt_ref[hd, 0, HEAD_DIM:, :] = ones
        pair = naq[:, (hd // 2) * V7X_LANES:(hd // 2 + 1) * V7X_LANES]
        naq_ref[hd] = jnp.where(lane_half == hd % 2, pair, 0.0).astype(BF16)
    nak = proj(_C_NAK, _C_END).astype(BF16)
    for pr in range(HEADS // 2):
        nak_ref[pr] = nak[:, pr * V7X_LANES:(pr + 1) * V7X_LANES]
    navt = lax.dot_general(wnavt_ref[...], h, _NT, preferred_element_type=F32).astype(BF16)
    for hd in range(HEADS):
        navt_ref[hd * V_ROWS:hd * V_ROWS + HEAD_DIM, :] = navt[hd * HEAD_DIM:(hd + 1) * HEAD_DIM, :]
        navt_ref[hd * V_ROWS + HEAD_DIM:(hd + 1) * V_ROWS, :] = ones


def _proj_call(x2d, seq, w):
    n = x2d.shape[0]
    tm = TOKEN_TILE
    n_tiles = n // tm
    pos_tiles = seq // tm
    const = lambda i: (0, 0)
    single = dict(pipeline_mode=pl.Buffered(1))
    in_specs = [
        pl.BlockSpec((tm, D_MODEL), lambda i: (i, 0)),
        pl.BlockSpec((1, D_MODEL), const),
        pl.BlockSpec((D_MODEL, _C_END), const, **single),
        pl.BlockSpec((1, Q_LORA_RANK), const),
        pl.BlockSpec((1, KV_LORA_RANK), const),
        pl.BlockSpec((Q_LORA_RANK, HEADS * V7X_LANES), const, **single),
        pl.BlockSpec((KV_LORA_RANK, HEADS * V7X_LANES), const, **single),
        pl.BlockSpec((MLA_WIDTH, KV_LORA_RANK), const, **single),
        pl.BlockSpec((NA_WIDTH, D_MODEL), const, **single),
        pl.BlockSpec((tm, V7X_LANES), lambda i: (i % pos_tiles, 0)),
        pl.BlockSpec((tm, V7X_LANES), lambda i: (i % pos_tiles, 0)),
        pl.BlockSpec((tm, V7X_LANES), lambda i: (i % pos_tiles, 0)),
    ]
    out_shape = (
        jax.ShapeDtypeStruct((HEADS, n, V7X_LANES), BF16),
        jax.ShapeDtypeStruct((HEADS, n, V7X_LANES), BF16),
        jax.ShapeDtypeStruct((HEADS, n_tiles, V_ROWS, tm), BF16),
        jax.ShapeDtypeStruct((HEADS, n, V7X_LANES), BF16),
        jax.ShapeDtypeStruct((HEADS // 2, n, V7X_LANES), BF16),
        jax.ShapeDtypeStruct((HEADS * V_ROWS, n), BF16),
    )
    out_specs = (
        pl.BlockSpec((HEADS, tm, V7X_LANES), lambda i: (0, i, 0)),
        pl.BlockSpec((HEADS, tm, V7X_LANES), lambda i: (0, i, 0)),
        pl.BlockSpec((HEADS, 1, V_ROWS, tm), lambda i: (0, i, 0, 0)),
        pl.BlockSpec((HEADS, tm, V7X_LANES), lambda i: (0, i, 0)),
        pl.BlockSpec((HEADS // 2, tm, V7X_LANES), lambda i: (0, i, 0)),
        pl.BlockSpec((HEADS * V_ROWS, tm), lambda i: (0, i)),
    )
    return pl.pallas_call(
        _proj_kernel,
        grid=(n_tiles,),
        in_specs=in_specs,
        out_specs=out_specs,
        out_shape=out_shape,
        compiler_params=pltpu.CompilerParams(
            dimension_semantics=("arbitrary",), vmem_limit_bytes=48 * 1024 * 1024),
        name="proj",
    )(x2d, w["attn_g"], w["w_main"], w["q_g"], w["kv_g"], w["w_uq"], w["w_k"], w["w_vt"],
      w["w_navt"], w["cos"], w["sin_lo"], w["sin_hi"])


def _mla_kernel(q_ref, k_ref, vt_ref, o_ref, acc_ref, s_ref, *, n_q, n_sc, n_u, group):
    tq, tk = MLA_Q_TILE, TOKEN_TILE
    acc_ref[...] = jnp.zeros_like(acc_ref)

    def split(u):
        return (u // n_sc, u % n_sc) if n_sc > 1 else (u, 0)

    def produce(u, qi):
        hd, sc = split(u)
        q = q_ref[hd, qi * tq:(qi + 1) * tq, :]
        col_max = []
        for c in range(group):
            k0 = pl.multiple_of((sc * group + c) * tk, tk)
            s = lax.dot_general(k_ref[hd, pl.ds(k0, tk), :], q, _NT, preferred_element_type=F32)
            s_ref[qi % 2, c] = s
            col_max.append(jnp.max(s, axis=0, keepdims=True))
        return functools.reduce(jnp.maximum, col_max)

    def consume(u, qi, m, cmax):
        hd, sc = split(u)
        m = jnp.where(sc == 0, NEG_INF, m)
        m_new = jnp.maximum(m, cmax)
        alpha = jnp.exp2(m - m_new)
        pv = 0.0
        for c in range(group):
            p = jnp.exp2(s_ref[qi % 2, c] - m_new).astype(BF16)
            pv = pv + jnp.dot(vt_ref[hd, sc * group + c], p, preferred_element_type=F32)
        acc = alpha * acc_ref[qi] + pv
        acc_ref[qi] = acc
        o_ref[_rows(hd * HEAD_DIM, HEAD_DIM), qi * tq:(qi + 1) * tq] = (
            acc[:HEAD_DIM] / acc[HEAD_DIM:HEAD_DIM + 1])
        return m_new

    def sweep(u, ms, cmax, last):
        new_ms = []
        for qi in range(n_q):
            if qi + 1 < n_q:
                cmax_next = produce(u, qi + 1)
            else:
                cmax_next = None if last else produce(u + 1, 0)
            new_ms.append(consume(u, qi, ms[qi], cmax))
            cmax = cmax_next
        return tuple(new_ms), cmax

    ms = tuple(jnp.full((1, tq), NEG_INF, F32) for _ in range(n_q))
    carry = lax.fori_loop(0, n_u - 1, lambda u, c: sweep(u, *c, False), (ms, produce(0, 0)))
    sweep(n_u - 1, *carry, True)


def _mla_call(q, k, vt, batch, seq):
    n = q.shape[1]
    qb = min(seq, MLA_Q_BLOCK)
    n_qb = seq // qb
    n_kv = seq // TOKEN_TILE
    n_q = qb // MLA_Q_TILE
    group = math.gcd(n_kv, MLA_KV_GROUP)
    n_sc = n_kv // group
    hb = max(1, min(HEADS, MLA_MIN_SWEEPS // n_sc))
    assert n_q % 2 == 0 and HEADS % hb == 0
    kernel = functools.partial(_mla_kernel, n_q=n_q, n_sc=n_sc, n_u=hb * n_sc, group=group)
    return pl.pallas_call(
        kernel,
        grid=(batch, HEADS // hb, n_qb),
        in_specs=[
            pl.BlockSpec((hb, qb, V7X_LANES), lambda b, h, i: (h, b * n_qb + i, 0)),
            pl.BlockSpec((hb, seq, V7X_LANES), lambda b, h, i: (h, b, 0)),
            pl.BlockSpec((hb, n_kv, V_ROWS, TOKEN_TILE), lambda b, h, i: (h, b, 0, 0)),
        ],
        out_specs=pl.BlockSpec((hb * HEAD_DIM, qb), lambda b, h, i: (h, b * n_qb + i)),
        out_shape=jax.ShapeDtypeStruct((MLA_WIDTH, n), F32),
        scratch_shapes=[pltpu.VMEM((n_q, V_ROWS, MLA_Q_TILE), F32),
                        pltpu.VMEM((2, group, TOKEN_TILE, MLA_Q_TILE), F32)],
        compiler_params=pltpu.CompilerParams(
            dimension_semantics=("arbitrary", "arbitrary", "arbitrary"),
            vmem_limit_bytes=40 * 1024 * 1024),
        name="mla",
    )(q, k, vt)


def _na_kernel(q_ref, *refs, groups):
    n_win = NA_GROUPS + NA_CHUNKS - 1
    k_refs, v_refs = refs[:n_win], refs[n_win:2 * n_win]
    bm_ref, o_ref, s_ref = refs[2 * n_win:]
    first_group = NA_GROUPS * pl.program_id(1)
    kinds = [jnp.where(first_group + e == 0, 0, jnp.where(first_group + e == groups - 1, 2, 1))
             for e in range(NA_GROUPS)]

    def produce(hd, e):
        q = q_ref[hd, e * NA_TOK:(e + 1) * NA_TOK, :]
        col_max = []
        for t in range(NA_CHUNKS):
            s = (lax.dot_general(k_refs[e + t][hd // 2], q, _NT, preferred_element_type=F32)
                 + bm_ref[kinds[e], hd, t])
            s_ref[e % 2, t] = s
            col_max.append(jnp.max(s, axis=0, keepdims=True))
        return functools.reduce(jnp.maximum, col_max)

    def consume(hd, e, m):
        v_rows = _rows(hd * V_ROWS, V_ROWS)
        o = 0.0
        for t in range(NA_CHUNKS):
            p = jnp.exp2(s_ref[e % 2, t] - m).astype(BF16)
            o = o + jnp.dot(v_refs[e + t][v_rows, :], p, preferred_element_type=F32)
        o_ref[_rows(hd * HEAD_DIM, HEAD_DIM), e * NA_TOK:(e + 1) * NA_TOK] = (
            o[:HEAD_DIM] / o[HEAD_DIM:HEAD_DIM + 1])

    def head_pair(hd, m, last):
        items = [(h, e) for h in (hd, hd + 1) for e in range(NA_GROUPS)]
        for n, item in enumerate(items):
            if n + 1 < len(items):
                m_next = produce(*items[n + 1])
            else:
                m_next = None if last else produce(hd + 2, 0)
            consume(*item, m)
            m = m_next
        return m

    m = lax.fori_loop(0, HEADS // 2 - 1, lambda i, m: head_pair(2 * i, m, False), produce(0, 0))
    head_pair(HEADS - 2, m, True)


def _na_call(naq, nak, navt, bias_mask, batch, seq):
    n = nak.shape[1]
    groups = seq // NA_TOK
    steps = groups // NA_GROUPS
    assert groups % NA_GROUPS == 0 and NA_GROUPS % 2 == 0

    def chunk(b, i, t):
        return b * groups + jnp.clip(NA_GROUPS * i - 1 + t, 0, groups - 1)

    n_win = NA_GROUPS + NA_CHUNKS - 1
    k_specs = [pl.BlockSpec((HEADS // 2, NA_TOK, V7X_LANES),
                            functools.partial(lambda b, i, t: (0, chunk(b, i, t), 0), t=t))
               for t in range(n_win)]
    v_specs = [pl.BlockSpec((HEADS * V_ROWS, NA_TOK),
                            functools.partial(lambda b, i, t: (0, chunk(b, i, t)), t=t))
               for t in range(n_win)]
    return pl.pallas_call(
        functools.partial(_na_kernel, groups=groups),
        grid=(batch, steps),
        in_specs=[pl.BlockSpec((HEADS, NA_GROUPS * NA_TOK, V7X_LANES),
                               lambda b, i: (0, b * steps + i, 0))]
        + k_specs + v_specs
        + [pl.BlockSpec((3, HEADS, NA_CHUNKS, NA_TOK, NA_TOK), lambda b, i: (0, 0, 0, 0, 0),
                        pipeline_mode=pl.Buffered(1))],
        out_specs=pl.BlockSpec((NA_WIDTH, NA_GROUPS * NA_TOK), lambda b, i: (0, b * steps + i)),
        out_shape=jax.ShapeDtypeStruct((NA_WIDTH, n), F32),
        scratch_shapes=[pltpu.VMEM((2, NA_CHUNKS, NA_TOK, NA_TOK), F32)],
        compiler_params=pltpu.CompilerParams(
            dimension_semantics=("arbitrary", "arbitrary"), vmem_limit_bytes=48 * 1024 * 1024),
        name="na",
    )(naq, *([nak] * n_win), *([navt] * n_win), bias_mask)


def _post_kernel(at_ref, bt_ref, x_ref, ga_ref, gb_ref, wo_ref, gf_ref, wg_ref, wu_ref, wd_ref,
                 gl_ref, o_ref, acc_ref, gu_ref):
    mix_t = jnp.concatenate([_rms_cols(at_ref[...], ga_ref[...]),
                             _rms_cols(bt_ref[...], gb_ref[...])], axis=0).astype(BF16)
    x1 = x_ref[...] + lax.dot_general(mix_t, wo_ref[...], _TN, preferred_element_type=F32)
    h2 = _rms_rows(x1, gf_ref[...]).astype(BF16)
    acc_ref[...] = x1

    def produce(c, slot):
        gu_ref[slot, 0] = jnp.dot(h2, wg_ref[c], preferred_element_type=F32)
        gu_ref[slot, 1] = jnp.dot(h2, wu_ref[c], preferred_element_type=F32)

    def consume(c, slot):
        gate = gu_ref[slot, 0]
        act = (gate * jax.nn.sigmoid(gate) * gu_ref[slot, 1]).astype(BF16)
        acc_ref[...] += jnp.dot(act, wd_ref[c], preferred_element_type=F32)

    def ff_pair(jj, _):
        c = 2 * jj
        produce(c + 1, 1)
        consume(c, 0)
        produce(c + 2, 0)
        consume(c + 1, 1)
        return 0

    n_ff = D_FF // FF_CHUNK
    assert n_ff % 2 == 1
    produce(0, 0)
    lax.fori_loop(0, n_ff // 2, ff_pair, 0)
    consume(n_ff - 1, 0)
    o_ref[...] = _rms_rows(acc_ref[...], gl_ref[...])


def _post_call(at, bt, x2d, w):
    n = x2d.shape[0]
    tm = TOKEN_TILE
    n_ff = D_FF // FF_CHUNK
    const2 = lambda i: (0, 0)
    const3 = lambda i: (0, 0, 0)
    single = dict(pipeline_mode=pl.Buffered(1))
    return pl.pallas_call(
        _post_kernel,
        grid=(n // tm,),
        in_specs=[
            pl.BlockSpec((MLA_WIDTH, tm), lambda i: (0, i)),
            pl.BlockSpec((NA_WIDTH, tm), lambda i: (0, i)),
            pl.BlockSpec((tm, D_MODEL), lambda i: (i, 0)),
            pl.BlockSpec((MLA_WIDTH, 1), const2),
            pl.BlockSpec((NA_WIDTH, 1), const2),
            pl.BlockSpec((D_MODEL, D_MODEL), const2, **single),
            pl.BlockSpec((1, D_MODEL), const2),
            pl.BlockSpec((n_ff, D_MODEL, FF_CHUNK), const3, **single),
            pl.BlockSpec((n_ff, D_MODEL, FF_CHUNK), const3, **single),
            pl.BlockSpec((n_ff, FF_CHUNK, D_MODEL), const3, **single),
            pl.BlockSpec((1, D_MODEL), const2),
        ],
        out_specs=pl.BlockSpec((tm, D_MODEL), lambda i: (i, 0)),
        out_shape=jax.ShapeDtypeStruct((n, D_MODEL), F32),
        scratch_shapes=[pltpu.VMEM((tm, D_MODEL), F32), pltpu.VMEM((2, 2, tm, FF_CHUNK), F32)],
        compiler_params=pltpu.CompilerParams(
            dimension_semantics=("arbitrary",), vmem_limit_bytes=56 * 1024 * 1024),
        name="post",
    )(at, bt, x2d, w["mla_out_g"], w["na_out_g"], w["w_o"], w["ffn_g"], w["w_gate"], w["w_up"],
      w["w_down"], w["final_g"])


def _pad_heads(wm, width):
    r = wm.shape[0]
    wm = wm.reshape(r, HEADS, width)
    return jnp.pad(wm, ((0, 0), (0, 0), (0, V7X_LANES - width))).reshape(r, HEADS * V7X_LANES)


def _prep_layer(attn_norm_g, w_in, q_norm_g, kv_norm_g, w_uq, w_ukv, na_rpb, mla_out_g, na_out_g,
                w_o, ffn_norm_g, w_gate, w_up, w_down, final_norm_g):
    o_kr = Q_LORA_RANK + KV_LORA_RANK
    o_na = o_kr + MLA_ROPE_DIM
    w_kr = w_in[:, o_kr:o_na]
    kr_block = jnp.zeros((D_MODEL, V7X_LANES), F32).at[:, HEAD_DIM:HEAD_DIM + MLA_ROPE_DIM].set(w_kr)
    w_nav = w_in[:, o_na + 2 * NA_WIDTH:]
    w_main = jnp.concatenate([w_in[:, :o_kr], kr_block, w_in[:, o_na:o_na + 2 * NA_WIDTH]],
                             axis=1).astype(BF16)

    w_ukv_h = w_ukv.reshape(KV_LORA_RANK, HEADS, 2 * HEAD_DIM)
    w_k = _pad_heads(w_ukv_h[:, :, :HEAD_DIM].reshape(KV_LORA_RANK, MLA_WIDTH), HEAD_DIM)
    w_v = w_ukv_h[:, :, HEAD_DIM:].reshape(KV_LORA_RANK, MLA_WIDTH)
    n_ff = D_FF // FF_CHUNK
    return {
        "attn_g": attn_norm_g.reshape(1, D_MODEL),
        "w_main": w_main,
        "q_g": q_norm_g.reshape(1, Q_LORA_RANK),
        "kv_g": kv_norm_g.reshape(1, KV_LORA_RANK),
        "w_uq": _pad_heads(w_uq, MLA_QK_DIM).astype(BF16),
        "w_k": w_k.astype(BF16),
        "w_vt": w_v.T.astype(BF16),
        "w_navt": w_nav.T.astype(BF16),
        "rpb": na_rpb,
        "mla_out_g": mla_out_g.reshape(MLA_WIDTH, 1),
        "na_out_g": na_out_g.reshape(NA_WIDTH, 1),
        "w_o": w_o.astype(BF16),
        "ffn_g": ffn_norm_g.reshape(1, D_MODEL),
        "w_gate": w_gate.reshape(D_MODEL, n_ff, FF_CHUNK).transpose(1, 0, 2).astype(BF16),
        "w_up": w_up.reshape(D_MODEL, n_ff, FF_CHUNK).transpose(1, 0, 2).astype(BF16),
        "w_down": w_down.reshape(n_ff, FF_CHUNK, D_MODEL).astype(BF16),
        "final_g": final_norm_g.reshape(1, D_MODEL),
    }


def _rope_tables(seq):
    half = MLA_ROPE_DIM // 2
    inv = ROPE_THETA ** (-jnp.arange(0, MLA_ROPE_DIM, 2, dtype=F32) / MLA_ROPE_DIM)
    ang = jnp.arange(seq, dtype=F32)[:, None] * inv[None, :]
    cos, sin = jnp.cos(ang), jnp.sin(ang)
    zeros = jnp.zeros((seq, half), F32)
    ones = jnp.ones((seq, HEAD_DIM), F32)
    tail = jnp.zeros((seq, V7X_LANES - MLA_QK_DIM), F32)
    cos_t = jnp.concatenate([ones, cos, cos, tail], axis=1)
    lead = jnp.zeros((seq, HEAD_DIM), F32)
    sin_lo = jnp.concatenate([lead, -sin, zeros, tail], axis=1)
    sin_hi = jnp.concatenate([lead, zeros, sin, tail], axis=1)
    return cos_t, sin_lo, sin_hi


def _na_bias_mask(rpb):
    qc = np.arange(GRID_W)[None, :]
    kc = np.arange(GRID_W)[:, None]
    col_start = np.clip(qc - NA_KW // 2, 0, GRID_W - NA_KW)
    col_ok = (kc >= col_start) & (kc < col_start + NA_KW)
    period = 2 * GRID_W
    n_d = 2 * NA_KH - 1
    rpb = rpb * LOG2_E
    row = jnp.concatenate([rpb[..., NA_KW - 1::-1],
                           jnp.zeros((HEADS, n_d, period - (2 * NA_KW - 1)), F32),
                           rpb[..., :NA_KW - 1:-1]], axis=-1)
    skew = jnp.broadcast_to(row[:, :, None, :], (HEADS, n_d, GRID_W, period))
    skew = skew.reshape(HEADS, n_d, GRID_W * period)[..., :GRID_W * (period - 1)]
    toe = skew.reshape(HEADS, n_d, GRID_W, period - 1)[..., :GRID_W]

    i = np.arange(NA_CHUNKS * NA_ROWS)[:, None]
    j = np.arange(NA_ROWS)[None, :]
    di = np.clip(i - NA_ROWS - j + NA_KH - 1, 0, n_d - 1)
    row_ok = np.stack([(i >= NA_ROWS) & (j >= 0),
                       (i - j >= 0) & (i - j < NA_KH),
                       (i < NA_KH) & (j >= 0)])
    ok = row_ok[:, :, None, :, None] & col_ok[None, None, :, None, :]
    bias = toe[:, di].transpose(0, 1, 3, 2, 4)
    bm = jnp.where(ok[:, None], bias[None], NEG_INF)
    return bm.reshape(3, HEADS, NA_CHUNKS, NA_TOK, NA_TOK)


def _run_trunk(x, w, bias_mask):
    batch, seq, _ = x.shape
    assert seq % TOKEN_TILE == 0 and seq % (NA_GROUPS * NA_TOK) == 0
    assert seq % min(seq, MLA_Q_BLOCK) == 0
    x2d = x.reshape(batch * seq, D_MODEL)
    cos_t, sin_lo, sin_hi = _rope_tables(seq)
    wt = dict(w, cos=cos_t, sin_lo=sin_lo, sin_hi=sin_hi)
    q, k, vt, naq, nak, navt = _proj_call(x2d, seq, wt)
    at = _mla_call(q, k, vt, batch, seq)
    bt = _na_call(naq, nak, navt, bias_mask, batch, seq)
    y = _post_call(at, bt, x2d, w)
    return y.reshape(batch, seq, D_MODEL)


def kernel(x_prompt, x_sample, attn_norm_g, w_in, q_norm_g, kv_norm_g, w_uq, w_ukv, na_rpb,
           mla_out_g, na_out_g, w_o, ffn_norm_g, w_gate, w_up, w_down, final_norm_g):
    assert attn_norm_g.shape[0] == 1, "single-layer trunk"
    w = _prep_layer(attn_norm_g[0], w_in[0], q_norm_g[0], kv_norm_g[0], w_uq[0], w_ukv[0], na_rpb[0],
                    mla_out_g[0], na_out_g[0], w_o[0], ffn_norm_g[0], w_gate[0], w_up[0], w_down[0],
                    final_norm_g)
    bias_mask = _na_bias_mask(w["rpb"])
    return (_run_trunk(x_prompt, w, bias_mask), _run_trunk(x_sample, w, bias_mask))
```

```python
import functools
import math

import jax
import jax.numpy as jnp
import numpy as np
from jax import lax
from jax.experimental import pallas as pl
from jax.experimental.pallas import tpu as pltpu

D_MODEL = 1024
MLA_WIDTH = D_MODEL // 2
NA_WIDTH = D_MODEL - MLA_WIDTH
HEADS = 8
HEAD_DIM = 64
MLA_ROPE_DIM = 32
MLA_QK_DIM = HEAD_DIM + MLA_ROPE_DIM
Q_LORA_RANK = 384
KV_LORA_RANK = 256
ROPE_THETA = 10000.0
GRID_W = 64
NA_KH = 8
NA_KW = 16
D_FF = int(math.ceil(8 * D_MODEL / 3 / 256)) * 256
EPS = 1e-6
NEG_INF = -1e30
LOG2_E = math.log2(math.e)

V7X_LANES = 128
V7X_BF16_SUBLANES = 16
V7X_VMEM_BYTES = 64 * 1024 * 1024
VMEM_SPILL_ALLOWANCE = 8 * 1024 * 1024

V_ROWS = HEAD_DIM + V7X_BF16_SUBLANES

TOKEN_TILE = 512
MLA_Q_TILE = 256
MLA_Q_BLOCK = 2048
MLA_KV_GROUP = 4
MLA_MIN_SWEEPS = 8
NA_ROWS = 4
NA_TOK = NA_ROWS * GRID_W
NA_CHUNKS = 3
NA_GROUPS = 8
FF_CHUNK = 256

_C_Q = 0
_C_KV = _C_Q + Q_LORA_RANK
_C_KR = _C_KV + KV_LORA_RANK
_C_NAQ = _C_KR + V7X_LANES
_C_NAK = _C_NAQ + NA_WIDTH
_C_END = _C_NAK + NA_WIDTH

_NT = (((1,), (1,)), ((), ()))
_TN = (((0,), (0,)), ((), ()))
BF16 = jnp.bfloat16
F32 = jnp.float32


def _rms_rows(x, g):
    return x * lax.rsqrt(jnp.mean(x * x, axis=-1, keepdims=True) + EPS) * g


def _rms_cols(xt, g):
    return xt * lax.rsqrt(jnp.mean(xt * xt, axis=0, keepdims=True) + EPS) * g


def _rope_lanes(x, cos, sin_lo, sin_hi):
    half = MLA_ROPE_DIM // 2
    return (x * cos + pltpu.roll(x, V7X_LANES - half, 1) * sin_lo
            + pltpu.roll(x, half, 1) * sin_hi)


def _rows(start, size):
    return pl.ds(start if isinstance(start, int) else pl.multiple_of(start, size), size)


def _launch(body, name, grid, operands, in_specs, out_specs, out_shape, scratch_shapes=()):
    def window(spec, dtype):
        buffers = 1 if spec.pipeline_mode is not None else 2
        return buffers * math.prod(spec.block_shape) * jnp.dtype(dtype).itemsize

    multi = isinstance(out_shape, (tuple, list))
    outs = tuple(out_shape) if multi else (out_shape,)
    o_specs = tuple(out_specs) if multi else (out_specs,)
    vmem = (sum(window(s, a.dtype) for s, a in zip(in_specs, operands))
            + sum(window(s, o.dtype) for s, o in zip(o_specs, outs))
            + sum(math.prod(s.shape) * jnp.dtype(s.dtype).itemsize for s in scratch_shapes)
            + VMEM_SPILL_ALLOWANCE)
    assert vmem <= V7X_VMEM_BYTES, (name, vmem)
    return pl.pallas_call(
        body, grid=grid, in_specs=list(in_specs), out_specs=out_specs, out_shape=out_shape,
        scratch_shapes=list(scratch_shapes),
        compiler_params=pltpu.CompilerParams(
            dimension_semantics=("arbitrary",) * len(grid), vmem_limit_bytes=vmem),
        name=name,
    )(*operands)


def _proj_kernel(x_ref, ga_ref, wmain_ref, gq_ref, gkv_ref, wuq_ref, wk_ref, wvt_ref, wnavt_ref,
                 cos_ref, sinlo_ref, sinhi_ref,
                 q_ref, k_ref, vt_ref, naq_ref, nak_ref, navt_ref):
    h = _rms_rows(x_ref[...], ga_ref[...]).astype(BF16)

    def proj(lo, hi):
        return jnp.dot(h, wmain_ref[:, lo:hi], preferred_element_type=F32)

    cqn = _rms_rows(proj(_C_Q, _C_KV), gq_ref[...]).astype(BF16)
    ckvn = _rms_rows(proj(_C_KV, _C_KR), gkv_ref[...]).astype(BF16)
    cos, sin_lo, sin_hi = cos_ref[...], sinlo_ref[...], sinhi_ref[...]
    k_rope = _rope_lanes(proj(_C_KR, _C_NAQ), cos, sin_lo, sin_hi)

    q = jnp.dot(cqn, wuq_ref[...], preferred_element_type=F32)
    kn = jnp.dot(ckvn, wk_ref[...], preferred_element_type=F32)
    vt = lax.dot_general(wvt_ref[...], ckvn, _NT, preferred_element_type=F32)
    naq = proj(_C_NAQ, _C_NAK) * (HEAD_DIM ** -0.5 * LOG2_E)
    scale = MLA_QK_DIM ** -0.5 * LOG2_E
    ones = jnp.ones((V_ROWS - HEAD_DIM, vt.shape[1]), BF16)
    lane_half = lax.broadcasted_iota(jnp.int32, (1, V7X_LANES), 1) // HEAD_DIM
    for hd in range(HEADS):
        lanes = slice(hd * V7X_LANES, (hd + 1) * V7X_LANES)
        q_ref[hd] = (_rope_lanes(q[:, lanes], cos, sin_lo, sin_hi) * scale).astype(BF16)
        k_ref[hd] = (kn[:, lanes] + k_rope).astype(BF16)
        vt_ref[hd, 0, :HEAD_DIM, :] = vt[hd * HEAD_DIM:(hd + 1) * HEAD_DIM, :].astype(BF16)
        vt_ref[hd, 0, HEAD_DIM:, :] = ones
        pair = naq[:, (hd // 2) * V7X_LANES:(hd // 2 + 1) * V7X_LANES]
        naq_ref[hd] = jnp.where(lane_half == hd % 2, pair, 0.0).astype(BF16)
    nak = proj(_C_NAK, _C_END).astype(BF16)
    for pr in range(HEADS // 2):
        nak_ref[pr] = nak[:, pr * V7X_LANES:(pr + 1) * V7X_LANES]
    navt = lax.dot_general(wnavt_ref[...], h, _NT, preferred_element_type=F32).astype(BF16)
    for hd in range(HEADS):
        navt_ref[hd * V_ROWS:hd * V_ROWS + HEAD_DIM, :] = navt[hd * HEAD_DIM:(hd + 1) * HEAD_DIM, :]
        navt_ref[hd * V_ROWS + HEAD_DIM:(hd + 1) * V_ROWS, :] = ones


def _proj_call(x2d, seq, w):
    n = x2d.shape[0]
    tm = TOKEN_TILE
    n_tiles = n // tm
    pos_tiles = seq // tm
    const = lambda i: (0, 0)
    single = dict(pipeline_mode=pl.Buffered(1))
    in_specs = [
        pl.BlockSpec((tm, D_MODEL), lambda i: (i, 0)),
        pl.BlockSpec((1, D_MODEL), const),
        pl.BlockSpec((D_MODEL, _C_END), const, **single),
        pl.BlockSpec((1, Q_LORA_RANK), const),
        pl.BlockSpec((1, KV_LORA_RANK), const),
        pl.BlockSpec((Q_LORA_RANK, HEADS * V7X_LANES), const, **single),
        pl.BlockSpec((KV_LORA_RANK, HEADS * V7X_LANES), const, **single),
        pl.BlockSpec((MLA_WIDTH, KV_LORA_RANK), const, **single),
        pl.BlockSpec((NA_WIDTH, D_MODEL), const, **single),
        pl.BlockSpec((tm, V7X_LANES), lambda i: (i % pos_tiles, 0)),
        pl.BlockSpec((tm, V7X_LANES), lambda i: (i % pos_tiles, 0)),
        pl.BlockSpec((tm, V7X_LANES), lambda i: (i % pos_tiles, 0)),
    ]
    out_shape = (
        jax.ShapeDtypeStruct((HEADS, n, V7X_LANES), BF16),
        jax.ShapeDtypeStruct((HEADS, n, V7X_LANES), BF16),
        jax.ShapeDtypeStruct((HEADS, n_tiles, V_ROWS, tm), BF16),
        jax.ShapeDtypeStruct((HEADS, n, V7X_LANES), BF16),
        jax.ShapeDtypeStruct((HEADS // 2, n, V7X_LANES), BF16),
        jax.ShapeDtypeStruct((HEADS * V_ROWS, n), BF16),
    )
    out_specs = (
        pl.BlockSpec((HEADS, tm, V7X_LANES), lambda i: (0, i, 0)),
        pl.BlockSpec((HEADS, tm, V7X_LANES), lambda i: (0, i, 0)),
        pl.BlockSpec((HEADS, 1, V_ROWS, tm), lambda i: (0, i, 0, 0)),
        pl.BlockSpec((HEADS, tm, V7X_LANES), lambda i: (0, i, 0)),
        pl.BlockSpec((HEADS // 2, tm, V7X_LANES), lambda i: (0, i, 0)),
        pl.BlockSpec((HEADS * V_ROWS, tm), lambda i: (0, i)),
    )
    operands = (x2d, w["attn_g"], w["w_main"], w["q_g"], w["kv_g"], w["w_uq"], w["w_k"], w["w_vt"],
                w["w_navt"], w["cos"], w["sin_lo"], w["sin_hi"])
    return _launch(_proj_kernel, "proj", (n_tiles,), operands, in_specs, out_specs, out_shape)


def _mla_kernel(q_ref, k_ref, vt_ref, o_ref, acc_ref, s_ref, *, n_q, n_sc, n_u, group):
    tq, tk = MLA_Q_TILE, TOKEN_TILE
    acc_ref[...] = jnp.zeros_like(acc_ref)

    def split(u):
        return (u // n_sc, u % n_sc) if n_sc > 1 else (u, 0)

    def produce(u, qi):
        hd, sc = split(u)
        q = q_ref[hd, qi * tq:(qi + 1) * tq, :]
        col_max = []
        for c in range(group):
            k0 = pl.multiple_of((sc * group + c) * tk, tk)
            s = lax.dot_general(k_ref[hd, pl.ds(k0, tk), :], q, _NT, preferred_element_type=F32)
            s_ref[qi % 2, c] = s
            col_max.append(jnp.max(s, axis=0, keepdims=True))
        return functools.reduce(jnp.maximum, col_max)

    def consume(u, qi, m, cmax):
        hd, sc = split(u)
        m = jnp.where(sc == 0, NEG_INF, m)
        m_new = jnp.maximum(m, cmax)
        alpha = jnp.exp2(m - m_new)
        pv = 0.0
        for c in range(group):
            p = jnp.exp2(s_ref[qi % 2, c] - m_new).astype(BF16)
            pv = pv + jnp.dot(vt_ref[hd, sc * group + c], p, preferred_element_type=F32)
        acc = alpha * acc_ref[qi] + pv
        acc_ref[qi] = acc
        o_ref[_rows(hd * HEAD_DIM, HEAD_DIM), qi * tq:(qi + 1) * tq] = (
            acc[:HEAD_DIM] / acc[HEAD_DIM:HEAD_DIM + 1])
        return m_new

    def sweep(u, ms, cmax, last):
        new_ms = []
        for qi in range(n_q):
            if qi + 1 < n_q:
                cmax_next = produce(u, qi + 1)
            else:
                cmax_next = None if last else produce(u + 1, 0)
            new_ms.append(consume(u, qi, ms[qi], cmax))
            cmax = cmax_next
        return tuple(new_ms), cmax

    ms = tuple(jnp.full((1, tq), NEG_INF, F32) for _ in range(n_q))
    carry = lax.fori_loop(0, n_u - 1, lambda u, c: sweep(u, *c, False), (ms, produce(0, 0)))
    sweep(n_u - 1, *carry, True)


def _mla_call(q, k, vt, batch, seq):
    n = q.shape[1]
    qb = min(seq, MLA_Q_BLOCK)
    n_qb = seq // qb
    n_kv = seq // TOKEN_TILE
    n_q = qb // MLA_Q_TILE
    group = math.gcd(n_kv, MLA_KV_GROUP)
    n_sc = n_kv // group
    hb = max(1, min(HEADS, MLA_MIN_SWEEPS // n_sc))
    assert n_q % 2 == 0 and HEADS % hb == 0
    kernel = functools.partial(_mla_kernel, n_q=n_q, n_sc=n_sc, n_u=hb * n_sc, group=group)
    in_specs = [
        pl.BlockSpec((hb, qb, V7X_LANES), lambda b, h, i: (h, b * n_qb + i, 0)),
        pl.BlockSpec((hb, seq, V7X_LANES), lambda b, h, i: (h, b, 0)),
        pl.BlockSpec((hb, n_kv, V_ROWS, TOKEN_TILE), lambda b, h, i: (h, b, 0, 0)),
    ]
    return _launch(
        kernel, "mla", (batch, HEADS // hb, n_qb), (q, k, vt), in_specs,
        pl.BlockSpec((hb * HEAD_DIM, qb), lambda b, h, i: (h, b * n_qb + i)),
        jax.ShapeDtypeStruct((MLA_WIDTH, n), F32),
        scratch_shapes=[pltpu.VMEM((n_q, V_ROWS, MLA_Q_TILE), F32),
                        pltpu.VMEM((2, group, TOKEN_TILE, MLA_Q_TILE), F32)])


def _na_kernel(q_ref, *refs, groups, ng):
    n_win = ng + NA_CHUNKS - 1
    k_refs, v_refs = refs[:n_win], refs[n_win:2 * n_win]
    bm_ref, o_ref, s_ref = refs[2 * n_win:]
    first_group = ng * pl.program_id(1)
    kinds = [jnp.where(first_group + e == 0, 0, jnp.where(first_group + e == groups - 1, 2, 1))
             for e in range(ng)]

    def produce(hd, e):
        q = q_ref[hd, e * NA_TOK:(e + 1) * NA_TOK, :]
        col_max = []
        for t in range(NA_CHUNKS):
            s = (lax.dot_general(k_refs[e + t][hd // 2], q, _NT, preferred_element_type=F32)
                 + bm_ref[kinds[e], hd, t])
            s_ref[e % 2, t] = s
            col_max.append(jnp.max(s, axis=0, keepdims=True))
        return functools.reduce(jnp.maximum, col_max)

    def consume(hd, e, m):
        v_rows = _rows(hd * V_ROWS, V_ROWS)
        o = 0.0
        for t in range(NA_CHUNKS):
            p = jnp.exp2(s_ref[e % 2, t] - m).astype(BF16)
            o = o + jnp.dot(v_refs[e + t][v_rows, :], p, preferred_element_type=F32)
        o_ref[_rows(hd * HEAD_DIM, HEAD_DIM), e * NA_TOK:(e + 1) * NA_TOK] = (
            o[:HEAD_DIM] / o[HEAD_DIM:HEAD_DIM + 1])

    def head_pair(hd, m, last):
        items = [(h, e) for h in (hd, hd + 1) for e in range(ng)]
        for n, item in enumerate(items):
            if n + 1 < len(items):
                m_next = produce(*items[n + 1])
            else:
                m_next = None if last else produce(hd + 2, 0)
            consume(*item, m)
            m = m_next
        return m

    m = lax.fori_loop(0, HEADS // 2 - 1, lambda i, m: head_pair(2 * i, m, False), produce(0, 0))
    head_pair(HEADS - 2, m, True)


def _na_call(naq, nak, navt, bias_mask, batch, seq):
    n = nak.shape[1]
    groups = seq // NA_TOK
    ng = math.gcd(groups, NA_GROUPS)
    steps = groups // ng
    assert ng % 2 == 0

    def chunk(b, i, t):
        return b * groups + jnp.clip(ng * i - 1 + t, 0, groups - 1)

    n_win = ng + NA_CHUNKS - 1
    k_specs = [pl.BlockSpec((HEADS // 2, NA_TOK, V7X_LANES),
                            functools.partial(lambda b, i, t: (0, chunk(b, i, t), 0), t=t))
               for t in range(n_win)]
    v_specs = [pl.BlockSpec((HEADS * V_ROWS, NA_TOK),
                            functools.partial(lambda b, i, t: (0, chunk(b, i, t)), t=t))
               for t in range(n_win)]
    in_specs = ([pl.BlockSpec((HEADS, ng * NA_TOK, V7X_LANES),
                              lambda b, i: (0, b * steps + i, 0))]
                + k_specs + v_specs
                + [pl.BlockSpec((3, HEADS, NA_CHUNKS, NA_TOK, NA_TOK), lambda b, i: (0, 0, 0, 0, 0),
                                pipeline_mode=pl.Buffered(1))])
    operands = (naq, *([nak] * n_win), *([navt] * n_win), bias_mask)
    return _launch(
        functools.partial(_na_kernel, groups=groups, ng=ng), "na", (batch, steps), operands, in_specs,
        pl.BlockSpec((NA_WIDTH, ng * NA_TOK), lambda b, i: (0, b * steps + i)),
        jax.ShapeDtypeStruct((NA_WIDTH, n), F32),
        scratch_shapes=[pltpu.VMEM((2, NA_CHUNKS, NA_TOK, NA_TOK), F32)])


def _post_kernel(at_ref, bt_ref, x_ref, ga_ref, gb_ref, wo_ref, gf_ref, wg_ref, wu_ref, wd_ref,
                 gl_ref, o_ref, acc_ref, gu_ref):
    mix_t = jnp.concatenate([_rms_cols(at_ref[...], ga_ref[...]),
                             _rms_cols(bt_ref[...], gb_ref[...])], axis=0).astype(BF16)
    x1 = x_ref[...] + lax.dot_general(mix_t, wo_ref[...], _TN, preferred_element_type=F32)
    h2 = _rms_rows(x1, gf_ref[...]).astype(BF16)
    acc_ref[...] = x1

    def produce(c, slot):
        gu_ref[slot, 0] = jnp.dot(h2, wg_ref[c], preferred_element_type=F32)
        gu_ref[slot, 1] = jnp.dot(h2, wu_ref[c], preferred_element_type=F32)

    def consume(c, slot):
        gate = gu_ref[slot, 0]
        act = (gate * jax.nn.sigmoid(gate) * gu_ref[slot, 1]).astype(BF16)
        acc_ref[...] += jnp.dot(act, wd_ref[c], preferred_element_type=F32)

    def ff_pair(jj, _):
        c = 2 * jj
        produce(c + 1, 1)
        consume(c, 0)
        produce(c + 2, 0)
        consume(c + 1, 1)
        return 0

    n_ff = D_FF // FF_CHUNK
    assert n_ff % 2 == 1
    produce(0, 0)
    lax.fori_loop(0, n_ff // 2, ff_pair, 0)
    consume(n_ff - 1, 0)
    o_ref[...] = _rms_rows(acc_ref[...], gl_ref[...])


def _post_call(at, bt, x2d, w):
    n = x2d.shape[0]
    tm = TOKEN_TILE
    n_ff = D_FF // FF_CHUNK
    const2 = lambda i: (0, 0)
    const3 = lambda i: (0, 0, 0)
    single = dict(pipeline_mode=pl.Buffered(1))
    in_specs = [
        pl.BlockSpec((MLA_WIDTH, tm), lambda i: (0, i)),
        pl.BlockSpec((NA_WIDTH, tm), lambda i: (0, i)),
        pl.BlockSpec((tm, D_MODEL), lambda i: (i, 0)),
        pl.BlockSpec((MLA_WIDTH, 1), const2),
        pl.BlockSpec((NA_WIDTH, 1), const2),
        pl.BlockSpec((D_MODEL, D_MODEL), const2, **single),
        pl.BlockSpec((1, D_MODEL), const2),
        pl.BlockSpec((n_ff, D_MODEL, FF_CHUNK), const3, **single),
        pl.BlockSpec((n_ff, D_MODEL, FF_CHUNK), const3, **single),
        pl.BlockSpec((n_ff, FF_CHUNK, D_MODEL), const3, **single),
        pl.BlockSpec((1, D_MODEL), const2),
    ]
    operands = (at, bt, x2d, w["mla_out_g"], w["na_out_g"], w["w_o"], w["ffn_g"], w["w_gate"],
                w["w_up"], w["w_down"], w["final_g"])
    return _launch(
        _post_kernel, "post", (n // tm,), operands, in_specs,
        pl.BlockSpec((tm, D_MODEL), lambda i: (i, 0)), jax.ShapeDtypeStruct((n, D_MODEL), F32),
        scratch_shapes=[pltpu.VMEM((tm, D_MODEL), F32), pltpu.VMEM((2, 2, tm, FF_CHUNK), F32)])


def _pad_heads(wm, width):
    r = wm.shape[0]
    wm = wm.reshape(r, HEADS, width)
    return jnp.pad(wm, ((0, 0), (0, 0), (0, V7X_LANES - width))).reshape(r, HEADS * V7X_LANES)


def _prep_layer(attn_norm_g, w_in, q_norm_g, kv_norm_g, w_uq, w_ukv, na_rpb, mla_out_g, na_out_g,
                w_o, ffn_norm_g, w_gate, w_up, w_down, final_norm_g):
    o_kr = Q_LORA_RANK + KV_LORA_RANK
    o_na = o_kr + MLA_ROPE_DIM
    w_kr = w_in[:, o_kr:o_na]
    kr_block = jnp.zeros((D_MODEL, V7X_LANES), F32).at[:, HEAD_DIM:HEAD_DIM + MLA_ROPE_DIM].set(w_kr)
    w_nav = w_in[:, o_na + 2 * NA_WIDTH:]
    w_main = jnp.concatenate([w_in[:, :o_kr], kr_block, w_in[:, o_na:o_na + 2 * NA_WIDTH]],
                             axis=1).astype(BF16)

    w_ukv_h = w_ukv.reshape(KV_LORA_RANK, HEADS, 2 * HEAD_DIM)
    w_k = _pad_heads(w_ukv_h[:, :, :HEAD_DIM].reshape(KV_LORA_RANK, MLA_WIDTH), HEAD_DIM)
    w_v = w_ukv_h[:, :, HEAD_DIM:].reshape(KV_LORA_RANK, MLA_WIDTH)
    n_ff = D_FF // FF_CHUNK
    return {
        "attn_g": attn_norm_g.reshape(1, D_MODEL),
        "w_main": w_main,
        "q_g": q_norm_g.reshape(1, Q_LORA_RANK),
        "kv_g": kv_norm_g.reshape(1, KV_LORA_RANK),
        "w_uq": _pad_heads(w_uq, MLA_QK_DIM).astype(BF16),
        "w_k": w_k.astype(BF16),
        "w_vt": w_v.T.astype(BF16),
        "w_navt": w_nav.T.astype(BF16),
        "rpb": na_rpb,
        "mla_out_g": mla_out_g.reshape(MLA_WIDTH, 1),
        "na_out_g": na_out_g.reshape(NA_WIDTH, 1),
        "w_o": w_o.astype(BF16),
        "ffn_g": ffn_norm_g.reshape(1, D_MODEL),
        "w_gate": w_gate.reshape(D_MODEL, n_ff, FF_CHUNK).transpose(1, 0, 2).astype(BF16),
        "w_up": w_up.reshape(D_MODEL, n_ff, FF_CHUNK).transpose(1, 0, 2).astype(BF16),
        "w_down": w_down.reshape(n_ff, FF_CHUNK, D_MODEL).astype(BF16),
        "final_g": final_norm_g.reshape(1, D_MODEL),
    }


def _rope_tables(seq):
    half = MLA_ROPE_DIM // 2
    inv = ROPE_THETA ** (-jnp.arange(0, MLA_ROPE_DIM, 2, dtype=F32) / MLA_ROPE_DIM)
    ang = jnp.arange(seq, dtype=F32)[:, None] * inv[None, :]
    cos, sin = jnp.cos(ang), jnp.sin(ang)
    zeros = jnp.zeros((seq, half), F32)
    ones = jnp.ones((seq, HEAD_DIM), F32)
    tail = jnp.zeros((seq, V7X_LANES - MLA_QK_DIM), F32)
    cos_t = jnp.concatenate([ones, cos, cos, tail], axis=1)
    lead = jnp.zeros((seq, HEAD_DIM), F32)
    sin_lo = jnp.concatenate([lead, -sin, zeros, tail], axis=1)
    sin_hi = jnp.concatenate([lead, zeros, sin, tail], axis=1)
    return cos_t, sin_lo, sin_hi


def _na_bias_mask(rpb):
    qc = np.arange(GRID_W)[None, :]
    kc = np.arange(GRID_W)[:, None]
    col_start = np.clip(qc - NA_KW // 2, 0, GRID_W - NA_KW)
    col_ok = (kc >= col_start) & (kc < col_start + NA_KW)
    period = 2 * GRID_W
    n_d = 2 * NA_KH - 1
    rpb = rpb * LOG2_E
    row = jnp.concatenate([rpb[..., NA_KW - 1::-1],
                           jnp.zeros((HEADS, n_d, period - (2 * NA_KW - 1)), F32),
                           rpb[..., :NA_KW - 1:-1]], axis=-1)
    skew = jnp.broadcast_to(row[:, :, None, :], (HEADS, n_d, GRID_W, period))
    skew = skew.reshape(HEADS, n_d, GRID_W * period)[..., :GRID_W * (period - 1)]
    toe = skew.reshape(HEADS, n_d, GRID_W, period - 1)[..., :GRID_W]

    i = np.arange(NA_CHUNKS * NA_ROWS)[:, None]
    j = np.arange(NA_ROWS)[None, :]
    di = np.clip(i - NA_ROWS - j + NA_KH - 1, 0, n_d - 1)
    row_ok = np.stack([(i >= NA_ROWS) & (j >= 0),
                       (i - j >= 0) & (i - j < NA_KH),
                       (i < NA_KH) & (j >= 0)])
    ok = row_ok[:, :, None, :, None] & col_ok[None, None, :, None, :]
    bias = toe[:, di].transpose(0, 1, 3, 2, 4)
    bm = jnp.where(ok[:, None], bias[None], NEG_INF)
    return bm.reshape(3, HEADS, NA_CHUNKS, NA_TOK, NA_TOK)


def _run_trunk(x, w, bias_mask):
    batch, seq, _ = x.shape
    assert seq % TOKEN_TILE == 0 and seq % (2 * NA_TOK) == 0
    assert seq % min(seq, MLA_Q_BLOCK) == 0
    x2d = x.reshape(batch * seq, D_MODEL)
    cos_t, sin_lo, sin_hi = _rope_tables(seq)
    wt = dict(w, cos=cos_t, sin_lo=sin_lo, sin_hi=sin_hi)
    q, k, vt, naq, nak, navt = _proj_call(x2d, seq, wt)
    at = _mla_call(q, k, vt, batch, seq)
    bt = _na_call(naq, nak, navt, bias_mask, batch, seq)
    y = _post_call(at, bt, x2d, w)
    return y.reshape(batch, seq, D_MODEL)


def kernel(x_prompt, x_sample, attn_norm_g, w_in, q_norm_g, kv_norm_g, w_uq, w_ukv, na_rpb,
           mla_out_g, na_out_g, w_o, ffn_norm_g, w_gate, w_up, w_down, final_norm_g):
    assert attn_norm_g.shape[0] == 1, "single-layer trunk"
    w = _prep_layer(attn_norm_g[0], w_in[0], q_norm_g[0], kv_norm_g[0], w_uq[0], w_ukv[0], na_rpb[0],
                    mla_out_g[0], na_out_g[0], w_o[0], ffn_norm_g[0], w_gate[0], w_up[0], w_down[0],
                    final_norm_g)
    bias_mask = _na_bias_mask(w["rpb"])
    return (_run_trunk(x_prompt, w, bias_mask), _run_trunk(x_sample, w, bias_mask))
```

```python
import functools
import math

import jax
import jax.numpy as jnp
import numpy as np
from jax import lax
from jax.experimental import pallas as pl
from jax.experimental.pallas import tpu as pltpu

D_MODEL = 1024
MLA_WIDTH = D_MODEL // 2
NA_WIDTH = D_MODEL - MLA_WIDTH
HEADS = 8
HEAD_DIM = 64
MLA_ROPE_DIM = 32
MLA_QK_DIM = HEAD_DIM + MLA_ROPE_DIM
Q_LORA_RANK = 384
KV_LORA_RANK = 256
ROPE_THETA = 10000.0
GRID_W = 64
NA_KH = 8
NA_KW = 16
D_FF = int(math.ceil(8 * D_MODEL / 3 / 256)) * 256
EPS = 1e-6
NEG_INF = -1e30
LOG2_E = math.log2(math.e)

V7X_LANES = 128
V7X_BF16_SUBLANES = 16
V7X_VMEM_BYTES = 64 * 1024 * 1024
VMEM_SPILL_ALLOWANCE = 8 * 1024 * 1024

V_ROWS = HEAD_DIM + V7X_BF16_SUBLANES

TOKEN_TILE = 512
MLA_Q_TILE = 256
MLA_Q_BLOCK = 2048
MLA_KV_GROUP = 4
MLA_MIN_SWEEPS = 8
NA_ROWS = 4
NA_TOK = NA_ROWS * GRID_W
NA_CHUNKS = 3
NA_GROUPS = 8
FF_CHUNK = 256

_C_Q = 0
_C_KV = _C_Q + Q_LORA_RANK
_C_KR = _C_KV + KV_LORA_RANK
_C_NAQ = _C_KR + V7X_LANES
_C_NAK = _C_NAQ + NA_WIDTH
_C_END = _C_NAK + NA_WIDTH

_NT = (((1,), (1,)), ((), ()))
_TN = (((0,), (0,)), ((), ()))
BF16 = jnp.bfloat16
F32 = jnp.float32


def _rms_rows(x, g):
    return x * lax.rsqrt(jnp.mean(x * x, axis=-1, keepdims=True) + EPS) * g


def _rms_cols(xt, g):
    return xt * lax.rsqrt(jnp.mean(xt * xt, axis=0, keepdims=True) + EPS) * g


def _rope_lanes(x, cos, sin_lo, sin_hi):
    half = MLA_ROPE_DIM // 2
    return (x * cos + pltpu.roll(x, V7X_LANES - half, 1) * sin_lo
            + pltpu.roll(x, half, 1) * sin_hi)


def _rows(start, size):
    return pl.ds(start if isinstance(start, int) else pl.multiple_of(start, size), size)


def _launch(body, name, grid, operands, in_specs, out_specs, out_shape, scratch_shapes=()):
    def window(spec, dtype):
        buffers = 1 if spec.pipeline_mode is not None else 2
        return buffers * math.prod(spec.block_shape) * jnp.dtype(dtype).itemsize

    multi = isinstance(out_shape, (tuple, list))
    outs = tuple(out_shape) if multi else (out_shape,)
    o_specs = tuple(out_specs) if multi else (out_specs,)
    vmem = (sum(window(s, a.dtype) for s, a in zip(in_specs, operands))
            + sum(window(s, o.dtype) for s, o in zip(o_specs, outs))
            + sum(math.prod(s.shape) * jnp.dtype(s.dtype).itemsize for s in scratch_shapes)
            + VMEM_SPILL_ALLOWANCE)
    assert vmem <= V7X_VMEM_BYTES, (name, vmem)
    return pl.pallas_call(
        body, grid=grid, in_specs=list(in_specs), out_specs=out_specs, out_shape=out_shape,
        scratch_shapes=list(scratch_shapes),
        compiler_params=pltpu.CompilerParams(
            dimension_semantics=("arbitrary",) * len(grid), vmem_limit_bytes=vmem),
        name=name,
    )(*operands)


def _proj_kernel(x_ref, ga_ref, wmain_ref, gq_ref, gkv_ref, wuq_ref, wk_ref, wvt_ref, wnavt_ref,
                 cos_ref, sinlo_ref, sinhi_ref,
                 q_ref, k_ref, vt_ref, naq_ref, nak_ref, navt_ref):
    h = _rms_rows(x_ref[...], ga_ref[...]).astype(BF16)

    def proj(lo, hi):
        return jnp.dot(h, wmain_ref[:, lo:hi], preferred_element_type=F32)

    cqn = _rms_rows(proj(_C_Q, _C_KV), gq_ref[...]).astype(BF16)
    ckvn = _rms_rows(proj(_C_KV, _C_KR), gkv_ref[...]).astype(BF16)
    cos, sin_lo, sin_hi = cos_ref[...], sinlo_ref[...], sinhi_ref[...]
    k_rope = _rope_lanes(proj(_C_KR, _C_NAQ), cos, sin_lo, sin_hi)

    q = jnp.dot(cqn, wuq_ref[...], preferred_element_type=F32)
    kn = jnp.dot(ckvn, wk_ref[...], preferred_element_type=F32)
    vt = lax.dot_general(wvt_ref[...], ckvn, _NT, preferred_element_type=F32)
    naq = proj(_C_NAQ, _C_NAK) * (HEAD_DIM ** -0.5 * LOG2_E)
    scale = MLA_QK_DIM ** -0.5 * LOG2_E
    ones = jnp.ones((V_ROWS - HEAD_DIM, vt.shape[1]), BF16)
    lane_half = lax.broadcasted_iota(jnp.int32, (1, V7X_LANES), 1) // HEAD_DIM
    for hd in range(HEADS):
        lanes = slice(hd * V7X_LANES, (hd + 1) * V7X_LANES)
        q_ref[hd] = (_rope_lanes(q[:, lanes], cos, sin_lo, sin_hi) * scale).astype(BF16)
        k_ref[hd] = (kn[:, lanes] + k_rope).astype(BF16)
        vt_ref[hd, 0, :HEAD_DIM, :] = vt[hd * HEAD_DIM:(hd + 1) * HEAD_DIM, :].astype(BF16)
        vt_ref[hd, 0, HEAD_DIM:, :] = ones
        pair = naq[:, (hd // 2) * V7X_LANES:(hd // 2 + 1) * V7X_LANES]
        naq_ref[hd] = jnp.where(lane_half == hd % 2, pair, 0.0).astype(BF16)
    nak = proj(_C_NAK, _C_END).astype(BF16)
    for pr in range(HEADS // 2):
        nak_ref[pr] = nak[:, pr * V7X_LANES:(pr + 1) * V7X_LANES]
    navt = lax.dot_general(wnavt_ref[...], h, _NT, preferred_element_type=F32).astype(BF16)
    for hd in range(HEADS):
        navt_ref[hd * V_ROWS:hd * V_ROWS + HEAD_DIM, :] = navt[hd * HEAD_DIM:(hd + 1) * HEAD_DIM, :]
        navt_ref[hd * V_ROWS + HEAD_DIM:(hd + 1) * V_ROWS, :] = ones


def _proj_call(x2d, seq, w):
    n = x2d.shape[0]
    tm = TOKEN_TILE
    n_tiles = n // tm
    pos_tiles = seq // tm
    const = lambda i: (0, 0)
    single = dict(pipeline_mode=pl.Buffered(1))
    in_specs = [
        pl.BlockSpec((tm, D_MODEL), lambda i: (i, 0)),
        pl.BlockSpec((1, D_MODEL), const),
        pl.BlockSpec((D_MODEL, _C_END), const, **single),
        pl.BlockSpec((1, Q_LORA_RANK), const),
        pl.BlockSpec((1, KV_LORA_RANK), const),
        pl.BlockSpec((Q_LORA_RANK, HEADS * V7X_LANES), const, **single),
        pl.BlockSpec((KV_LORA_RANK, HEADS * V7X_LANES), const, **single),
        pl.BlockSpec((MLA_WIDTH, KV_LORA_RANK), const, **single),
        pl.BlockSpec((NA_WIDTH, D_MODEL), const, **single),
        pl.BlockSpec((tm, V7X_LANES), lambda i: (i % pos_tiles, 0)),
        pl.BlockSpec((tm, V7X_LANES), lambda i: (i % pos_tiles, 0)),
        pl.BlockSpec((tm, V7X_LANES), lambda i: (i % pos_tiles, 0)),
    ]
    out_shape = (
        jax.ShapeDtypeStruct((HEADS, n, V7X_LANES), BF16),
        jax.ShapeDtypeStruct((HEADS, n, V7X_LANES), BF16),
        jax.ShapeDtypeStruct((HEADS, n_tiles, V_ROWS, tm), BF16),
        jax.ShapeDtypeStruct((HEADS, n, V7X_LANES), BF16),
        jax.ShapeDtypeStruct((HEADS // 2, n, V7X_LANES), BF16),
        jax.ShapeDtypeStruct((HEADS * V_ROWS, n), BF16),
    )
    out_specs = (
        pl.BlockSpec((HEADS, tm, V7X_LANES), lambda i: (0, i, 0)),
        pl.BlockSpec((HEADS, tm, V7X_LANES), lambda i: (0, i, 0)),
        pl.BlockSpec((HEADS, 1, V_ROWS, tm), lambda i: (0, i, 0, 0)),
        pl.BlockSpec((HEADS, tm, V7X_LANES), lambda i: (0, i, 0)),
        pl.BlockSpec((HEADS // 2, tm, V7X_LANES), lambda i: (0, i, 0)),
        pl.BlockSpec((HEADS * V_ROWS, tm), lambda i: (0, i)),
    )
    operands = (x2d, w["attn_g"], w["w_main"], w["q_g"], w["kv_g"], w["w_uq"], w["w_k"], w["w_vt"],
                w["w_navt"], w["cos"], w["sin_lo"], w["sin_hi"])
    return _launch(_proj_kernel, "proj", (n_tiles,), operands, in_specs, out_specs, out_shape)


def _mla_kernel(q_ref, k_ref, vt_ref, o_ref, acc_ref, s_ref, *, n_q, n_sc, n_u, group):
    tq, tk = MLA_Q_TILE, TOKEN_TILE
    acc_ref[...] = jnp.zeros_like(acc_ref)

    def split(u):
        return (u // n_sc, u % n_sc) if n_sc > 1 else (u, 0)

    def produce(u, qi):
        hd, sc = split(u)
        q = q_ref[hd, qi * tq:(qi + 1) * tq, :]
        col_max = []
        for c in range(group):
            k0 = pl.multiple_of((sc * group + c) * tk, tk)
            s = lax.dot_general(k_ref[hd, pl.ds(k0, tk), :], q, _NT, preferred_element_type=F32)
            s_ref[qi % 2, c] = s
            col_max.append(jnp.max(s, axis=0, keepdims=True))
        return functools.reduce(jnp.maximum, col_max)

    def consume(u, qi, m, cmax):
        hd, sc = split(u)
        m = jnp.where(sc == 0, NEG_INF, m)
        m_new = jnp.maximum(m, cmax)
        alpha = jnp.exp2(m - m_new)
        pv = 0.0
        for c in range(group):
            p = jnp.exp2(s_ref[qi % 2, c] - m_new).astype(BF16)
            pv = pv + jnp.dot(vt_ref[hd, sc * group + c], p, preferred_element_type=F32)
        acc = alpha * acc_ref[qi] + pv
        acc_ref[qi] = acc
        o_ref[_rows(hd * HEAD_DIM, HEAD_DIM), qi * tq:(qi + 1) * tq] = (
            acc[:HEAD_DIM] / acc[HEAD_DIM:HEAD_DIM + 1])
        return m_new

    def sweep(u, ms, cmax, last):
        new_ms = []
        for qi in range(n_q):
            if qi + 1 < n_q:
                cmax_next = produce(u, qi + 1)
            else:
                cmax_next = None if last else produce(u + 1, 0)
            new_ms.append(consume(u, qi, ms[qi], cmax))
            cmax = cmax_next
        return tuple(new_ms), cmax

    ms = tuple(jnp.full((1, tq), NEG_INF, F32) for _ in range(n_q))
    carry = lax.fori_loop(0, n_u - 1, lambda u, c: sweep(u, *c, False), (ms, produce(0, 0)))
    sweep(n_u - 1, *carry, True)


def _mla_call(q, k, vt, batch, seq):
    n = q.shape[1]
    qb = min(seq, MLA_Q_BLOCK)
    n_qb = seq // qb
    n_kv = seq // TOKEN_TILE
    n_q = qb // MLA_Q_TILE
    group = math.gcd(n_kv, MLA_KV_GROUP)
    n_sc = n_kv // group
    hb = max(1, min(HEADS, MLA_MIN_SWEEPS // n_sc))
    assert n_q % 2 == 0 and HEADS % hb == 0
    kernel = functools.partial(_mla_kernel, n_q=n_q, n_sc=n_sc, n_u=hb * n_sc, group=group)
    in_specs = [
        pl.BlockSpec((hb, qb, V7X_LANES), lambda b, h, i: (h, b * n_qb + i, 0)),
        pl.BlockSpec((hb, seq, V7X_LANES), lambda b, h, i: (h, b, 0)),
        pl.BlockSpec((hb, n_kv, V_ROWS, TOKEN_TILE), lambda b, h, i: (h, b, 0, 0)),
    ]
    return _launch(
        kernel, "mla", (batch, HEADS // hb, n_qb), (q, k, vt), in_specs,
        pl.BlockSpec((hb * HEAD_DIM, qb), lambda b, h, i: (h, b * n_qb + i)),
        jax.ShapeDtypeStruct((MLA_WIDTH, n), F32),
        scratch_shapes=[pltpu.VMEM((n_q, V_ROWS, MLA_Q_TILE), F32),
                        pltpu.VMEM((2, group, TOKEN_TILE, MLA_Q_TILE), F32)])


def _na_kernel(q_ref, *refs, groups, ng):
    n_win = ng + NA_CHUNKS - 1
    k_refs, v_refs = refs[:n_win], refs[n_win:2 * n_win]
    bm_ref, o_ref, s_ref = refs[2 * n_win:]
    first_group = ng * pl.program_id(1)
    kinds = [jnp.where(first_group + e == 0, 0, jnp.where(first_group + e == groups - 1, 2, 1))
             for e in range(ng)]

    def produce(hd, e):
        q = q_ref[hd, e * NA_TOK:(e + 1) * NA_TOK, :]
        col_max = []
        for t in range(NA_CHUNKS):
            s = (lax.dot_general(k_refs[e + t][hd // 2], q, _NT, preferred_element_type=F32)
                 + bm_ref[kinds[e], hd, t])
            s_ref[e % 2, t] = s
            col_max.append(jnp.max(s, axis=0, keepdims=True))
        return functools.reduce(jnp.maximum, col_max)

    def consume(hd, e, m):
        v_rows = _rows(hd * V_ROWS, V_ROWS)
        o = 0.0
        for t in range(NA_CHUNKS):
            p = jnp.exp2(s_ref[e % 2, t] - m).astype(BF16)
            o = o + jnp.dot(v_refs[e + t][v_rows, :], p, preferred_element_type=F32)
        o_ref[_rows(hd * HEAD_DIM, HEAD_DIM), e * NA_TOK:(e + 1) * NA_TOK] = (
            o[:HEAD_DIM] / o[HEAD_DIM:HEAD_DIM + 1])

    def head_pair(hd, m, last):
        items = [(h, e) for h in (hd, hd + 1) for e in range(ng)]
        for n, item in enumerate(items):
            if n + 1 < len(items):
                m_next = produce(*items[n + 1])
            else:
                m_next = None if last else produce(hd + 2, 0)
            consume(*item, m)
            m = m_next
        return m

    m = lax.fori_loop(0, HEADS // 2 - 1, lambda i, m: head_pair(2 * i, m, False), produce(0, 0))
    head_pair(HEADS - 2, m, True)


def _na_call(naq, nak, navt, bias_mask, batch, seq):
    n = nak.shape[1]
    groups = seq // NA_TOK
    ng = math.gcd(groups, NA_GROUPS)
    steps = groups // ng
    assert ng % 2 == 0

    def chunk(b, i, t):
        return b * groups + jnp.clip(ng * i - 1 + t, 0, groups - 1)

    n_win = ng + NA_CHUNKS - 1
    k_specs = [pl.BlockSpec((HEADS // 2, NA_TOK, V7X_LANES),
                            functools.partial(lambda b, i, t: (0, chunk(b, i, t), 0), t=t))
               for t in range(n_win)]
    v_specs = [pl.BlockSpec((HEADS * V_ROWS, NA_TOK),
                            functools.partial(lambda b, i, t: (0, chunk(b, i, t)), t=t))
               for t in range(n_win)]
    in_specs = ([pl.BlockSpec((HEADS, ng * NA_TOK, V7X_LANES),
                              lambda b, i: (0, b * steps + i, 0))]
                + k_specs + v_specs
                + [pl.BlockSpec((3, HEADS, NA_CHUNKS, NA_TOK, NA_TOK), lambda b, i: (0, 0, 0, 0, 0),
                                pipeline_mode=pl.Buffered(1))])
    operands = (naq, *([nak] * n_win), *([navt] * n_win), bias_mask)
    return _launch(
        functools.partial(_na_kernel, groups=groups, ng=ng), "na", (batch, steps), operands, in_specs,
        pl.BlockSpec((NA_WIDTH, ng * NA_TOK), lambda b, i: (0, b * steps + i)),
        jax.ShapeDtypeStruct((NA_WIDTH, n), F32),
        scratch_shapes=[pltpu.VMEM((2, NA_CHUNKS, NA_TOK, NA_TOK), F32)])


def _post_kernel(at_ref, bt_ref, x_ref, ga_ref, gb_ref, wo_ref, gf_ref, wg_ref, wu_ref, wd_ref,
                 gl_ref, o_ref, acc_ref, gu_ref):
    mix_t = jnp.concatenate([_rms_cols(at_ref[...], ga_ref[...]),
                             _rms_cols(bt_ref[...], gb_ref[...])], axis=0).astype(BF16)
    x1 = x_ref[...] + lax.dot_general(mix_t, wo_ref[...], _TN, preferred_element_type=F32)
    h2 = _rms_rows(x1, gf_ref[...]).astype(BF16)
    acc_ref[...] = x1

    def produce(c, slot):
        gu_ref[slot, 0] = jnp.dot(h2, wg_ref[c], preferred_element_type=F32)
        gu_ref[slot, 1] = jnp.dot(h2, wu_ref[c], preferred_element_type=F32)

    def consume(c, slot):
        gate = gu_ref[slot, 0]
        act = (gate * jax.nn.sigmoid(gate) * gu_ref[slot, 1]).astype(BF16)
        acc_ref[...] += jnp.dot(act, wd_ref[c], preferred_element_type=F32)

    n_ff = D_FF // FF_CHUNK
    produce(0, 0)
    for c in range(n_ff):
        if c + 1 < n_ff:
            produce(c + 1, (c + 1) % 2)
        consume(c, c % 2)
    o_ref[...] = _rms_rows(acc_ref[...], gl_ref[...])


def _post_call(at, bt, x2d, w):
    n = x2d.shape[0]
    tm = TOKEN_TILE
    n_ff = D_FF // FF_CHUNK
    const2 = lambda i: (0, 0)
    const3 = lambda i: (0, 0, 0)
    single = dict(pipeline_mode=pl.Buffered(1))
    in_specs = [
        pl.BlockSpec((MLA_WIDTH, tm), lambda i: (0, i)),
        pl.BlockSpec((NA_WIDTH, tm), lambda i: (0, i)),
        pl.BlockSpec((tm, D_MODEL), lambda i: (i, 0)),
        pl.BlockSpec((MLA_WIDTH, 1), const2),
        pl.BlockSpec((NA_WIDTH, 1), const2),
        pl.BlockSpec((D_MODEL, D_MODEL), const2, **single),
        pl.BlockSpec((1, D_MODEL), const2),
        pl.BlockSpec((n_ff, D_MODEL, FF_CHUNK), const3, **single),
        pl.BlockSpec((n_ff, D_MODEL, FF_CHUNK), const3, **single),
        pl.BlockSpec((n_ff, FF_CHUNK, D_MODEL), const3, **single),
        pl.BlockSpec((1, D_MODEL), const2),
    ]
    operands = (at, bt, x2d, w["mla_out_g"], w["na_out_g"], w["w_o"], w["ffn_g"], w["w_gate"],
                w["w_up"], w["w_down"], w["final_g"])
    return _launch(
        _post_kernel, "post", (n // tm,), operands, in_specs,
        pl.BlockSpec((tm, D_MODEL), lambda i: (i, 0)), jax.ShapeDtypeStruct((n, D_MODEL), F32),
        scratch_shapes=[pltpu.VMEM((tm, D_MODEL), F32), pltpu.VMEM((2, 2, tm, FF_CHUNK), F32)])


def _pad_heads(wm, width):
    r = wm.shape[0]
    wm = wm.reshape(r, HEADS, width)
    return jnp.pad(wm, ((0, 0), (0, 0), (0, V7X_LANES - width))).reshape(r, HEADS * V7X_LANES)


def _prep_layer(attn_norm_g, w_in, q_norm_g, kv_norm_g, w_uq, w_ukv, na_rpb, mla_out_g, na_out_g,
                w_o, ffn_norm_g, w_gate, w_up, w_down, final_norm_g):
    o_kr = Q_LORA_RANK + KV_LORA_RANK
    o_na = o_kr + MLA_ROPE_DIM
    w_kr = w_in[:, o_kr:o_na]
    kr_block = jnp.zeros((D_MODEL, V7X_LANES), F32).at[:, HEAD_DIM:HEAD_DIM + MLA_ROPE_DIM].set(w_kr)
    w_nav = w_in[:, o_na + 2 * NA_WIDTH:]
    w_main = jnp.concatenate([w_in[:, :o_kr], kr_block, w_in[:, o_na:o_na + 2 * NA_WIDTH]],
                             axis=1).astype(BF16)

    w_ukv_h = w_ukv.reshape(KV_LORA_RANK, HEADS, 2 * HEAD_DIM)
    w_k = _pad_heads(w_ukv_h[:, :, :HEAD_DIM].reshape(KV_LORA_RANK, MLA_WIDTH), HEAD_DIM)
    w_v = w_ukv_h[:, :, HEAD_DIM:].reshape(KV_LORA_RANK, MLA_WIDTH)
    n_ff = D_FF // FF_CHUNK
    return {
        "attn_g": attn_norm_g.reshape(1, D_MODEL),
        "w_main": w_main,
        "q_g": q_norm_g.reshape(1, Q_LORA_RANK),
        "kv_g": kv_norm_g.reshape(1, KV_LORA_RANK),
        "w_uq": _pad_heads(w_uq, MLA_QK_DIM).astype(BF16),
        "w_k": w_k.astype(BF16),
        "w_vt": w_v.T.astype(BF16),
        "w_navt": w_nav.T.astype(BF16),
        "rpb": na_rpb,
        "mla_out_g": mla_out_g.reshape(MLA_WIDTH, 1),
        "na_out_g": na_out_g.reshape(NA_WIDTH, 1),
        "w_o": w_o.astype(BF16),
        "ffn_g": ffn_norm_g.reshape(1, D_MODEL),
        "w_gate": w_gate.reshape(D_MODEL, n_ff, FF_CHUNK).transpose(1, 0, 2).astype(BF16),
        "w_up": w_up.reshape(D_MODEL, n_ff, FF_CHUNK).transpose(1, 0, 2).astype(BF16),
        "w_down": w_down.reshape(n_ff, FF_CHUNK, D_MODEL).astype(BF16),
        "final_g": final_norm_g.reshape(1, D_MODEL),
    }


def _rope_tables(seq):
    half = MLA_ROPE_DIM // 2
    inv = ROPE_THETA ** (-jnp.arange(0, MLA_ROPE_DIM, 2, dtype=F32) / MLA_ROPE_DIM)
    ang = jnp.arange(seq, dtype=F32)[:, None] * inv[None, :]
    cos, sin = jnp.cos(ang), jnp.sin(ang)
    zeros = jnp.zeros((seq, half), F32)
    ones = jnp.ones((seq, HEAD_DIM), F32)
    tail = jnp.zeros((seq, V7X_LANES - MLA_QK_DIM), F32)
    cos_t = jnp.concatenate([ones, cos, cos, tail], axis=1)
    lead = jnp.zeros((seq, HEAD_DIM), F32)
    sin_lo = jnp.concatenate([lead, -sin, zeros, tail], axis=1)
    sin_hi = jnp.concatenate([lead, zeros, sin, tail], axis=1)
    return cos_t, sin_lo, sin_hi


def _na_bias_mask(rpb):
    qc = np.arange(GRID_W)[None, :]
    kc = np.arange(GRID_W)[:, None]
    col_start = np.clip(qc - NA_KW // 2, 0, GRID_W - NA_KW)
    col_ok = (kc >= col_start) & (kc < col_start + NA_KW)
    period = 2 * GRID_W
    n_d = 2 * NA_KH - 1
    rpb = rpb * LOG2_E
    row = jnp.concatenate([rpb[..., NA_KW - 1::-1],
                           jnp.zeros((HEADS, n_d, period - (2 * NA_KW - 1)), F32),
                           rpb[..., :NA_KW - 1:-1]], axis=-1)
    skew = jnp.broadcast_to(row[:, :, None, :], (HEADS, n_d, GRID_W, period))
    skew = skew.reshape(HEADS, n_d, GRID_W * period)[..., :GRID_W * (period - 1)]
    toe = skew.reshape(HEADS, n_d, GRID_W, period - 1)[..., :GRID_W]

    i = np.arange(NA_CHUNKS * NA_ROWS)[:, None]
    j = np.arange(NA_ROWS)[None, :]
    di = np.clip(i - NA_ROWS - j + NA_KH - 1, 0, n_d - 1)
    row_ok = np.stack([(i >= NA_ROWS) & (j >= 0),
                       (i - j >= 0) & (i - j < NA_KH),
                       (i < NA_KH) & (j >= 0)])
    ok = row_ok[:, :, None, :, None] & col_ok[None, None, :, None, :]
    bias = toe[:, di].transpose(0, 1, 3, 2, 4)
    bm = jnp.where(ok[:, None], bias[None], NEG_INF)
    return bm.reshape(3, HEADS, NA_CHUNKS, NA_TOK, NA_TOK)


def _run_trunk(x, w, bias_mask):
    batch, seq, _ = x.shape
    assert seq % TOKEN_TILE == 0 and seq % (2 * NA_TOK) == 0
    assert seq % min(seq, MLA_Q_BLOCK) == 0
    x2d = x.reshape(batch * seq, D_MODEL)
    cos_t, sin_lo, sin_hi = _rope_tables(seq)
    wt = dict(w, cos=cos_t, sin_lo=sin_lo, sin_hi=sin_hi)
    q, k, vt, naq, nak, navt = _proj_call(x2d, seq, wt)
    at = _mla_call(q, k, vt, batch, seq)
    bt = _na_call(naq, nak, navt, bias_mask, batch, seq)
    y = _post_call(at, bt, x2d, w)
    return y.reshape(batch, seq, D_MODEL)


def kernel(x_prompt, x_sample, attn_norm_g, w_in, q_norm_g, kv_norm_g, w_uq, w_ukv, na_rpb,
           mla_out_g, na_out_g, w_o, ffn_norm_g, w_gate, w_up, w_down, final_norm_g):
    assert attn_norm_g.shape[0] == 1, "single-layer trunk"
    w = _prep_layer(attn_norm_g[0], w_in[0], q_norm_g[0], kv_norm_g[0], w_uq[0], w_ukv[0], na_rpb[0],
                    mla_out_g[0], na_out_g[0], w_o[0], ffn_norm_g[0], w_gate[0], w_up[0], w_down[0],
                    final_norm_g)
    bias_mask = _na_bias_mask(w["rpb"])
    return (_run_trunk(x_prompt, w, bias_mask), _run_trunk(x_sample, w, bias_mask))
```

```python
import functools
import math

import jax
import jax.numpy as jnp
import numpy as np
from jax import lax
from jax.experimental import pallas as pl
from jax.experimental.pallas import tpu as pltpu

D_MODEL = 1024
MLA_WIDTH = D_MODEL // 2
NA_WIDTH = D_MODEL - MLA_WIDTH
HEADS = 8
HEAD_DIM = 64
MLA_ROPE_DIM = 32
MLA_QK_DIM = HEAD_DIM + MLA_ROPE_DIM
Q_LORA_RANK = 384
KV_LORA_RANK = 256
ROPE_THETA = 10000.0
GRID_W = 64
NA_KH = 8
NA_KW = 16
D_FF = int(math.ceil(8 * D_MODEL / 3 / 256)) * 256
EPS = 1e-6
NEG_INF = -1e30
LOG2_E = math.log2(math.e)

V7X_LANES = 128
V7X_BF16_SUBLANES = 16
V7X_VMEM_BYTES = 64 * 1024 * 1024
VMEM_SPILL_ALLOWANCE = 8 * 1024 * 1024

V_ROWS = HEAD_DIM + V7X_BF16_SUBLANES

TOKEN_TILE = 512
MLA_Q_TILE = 256
MLA_Q_BLOCK = 2048
MLA_KV_GROUP = 4
MLA_MIN_SWEEPS = 8
NA_ROWS = 4
NA_TOK = NA_ROWS * GRID_W
NA_CHUNKS = 3
NA_GROUPS = 8
FF_CHUNK = 256

_C_Q = 0
_C_KV = _C_Q + Q_LORA_RANK
_C_KR = _C_KV + KV_LORA_RANK
_C_NAQ = _C_KR + V7X_LANES
_C_NAK = _C_NAQ + NA_WIDTH
_C_END = _C_NAK + NA_WIDTH

_NT = (((1,), (1,)), ((), ()))
_TN = (((0,), (0,)), ((), ()))
BF16 = jnp.bfloat16
F32 = jnp.float32


def _rms_rows(x, g):
    return x * lax.rsqrt(jnp.mean(x * x, axis=-1, keepdims=True) + EPS) * g


def _rms_cols(xt, g):
    return xt * lax.rsqrt(jnp.mean(xt * xt, axis=0, keepdims=True) + EPS) * g


def _rope_lanes(x, cos, sin_lo, sin_hi):
    half = MLA_ROPE_DIM // 2
    return (x * cos + pltpu.roll(x, V7X_LANES - half, 1) * sin_lo
            + pltpu.roll(x, half, 1) * sin_hi)


def _rows(start, size):
    return pl.ds(start if isinstance(start, int) else pl.multiple_of(start, size), size)


def _launch(body, name, grid, operands, in_specs, out_specs, out_shape, scratch_shapes=()):
    def window(spec, dtype):
        buffers = 1 if spec.pipeline_mode is not None else 2
        return buffers * math.prod(spec.block_shape) * jnp.dtype(dtype).itemsize

    multi = isinstance(out_shape, (tuple, list))
    outs = tuple(out_shape) if multi else (out_shape,)
    o_specs = tuple(out_specs) if multi else (out_specs,)
    vmem = (sum(window(s, a.dtype) for s, a in zip(in_specs, operands))
            + sum(window(s, o.dtype) for s, o in zip(o_specs, outs))
            + sum(math.prod(s.shape) * jnp.dtype(s.dtype).itemsize for s in scratch_shapes)
            + VMEM_SPILL_ALLOWANCE)
    assert vmem <= V7X_VMEM_BYTES, (name, vmem)
    return pl.pallas_call(
        body, grid=grid, in_specs=list(in_specs), out_specs=out_specs, out_shape=out_shape,
        scratch_shapes=list(scratch_shapes),
        compiler_params=pltpu.CompilerParams(
            dimension_semantics=("arbitrary",) * len(grid), vmem_limit_bytes=vmem),
        name=name,
    )(*operands)


def _proj_kernel(x_ref, ga_ref, wmain_ref, gq_ref, gkv_ref, wuq_ref, wk_ref, wvt_ref, wnavt_ref,
                 cos_ref, sinlo_ref, sinhi_ref,
                 q_ref, k_ref, vt_ref, naq_ref, nak_ref, navt_ref):
    h = _rms_rows(x_ref[...], ga_ref[...]).astype(BF16)

    def proj(lo, hi):
        return jnp.dot(h, wmain_ref[:, lo:hi], preferred_element_type=F32)

    cqn = _rms_rows(proj(_C_Q, _C_KV), gq_ref[...]).astype(BF16)
    ckvn = _rms_rows(proj(_C_KV, _C_KR), gkv_ref[...]).astype(BF16)
    cos, sin_lo, sin_hi = cos_ref[...], sinlo_ref[...], sinhi_ref[...]
    k_rope = _rope_lanes(proj(_C_KR, _C_NAQ), cos, sin_lo, sin_hi)

    q = jnp.dot(cqn, wuq_ref[...], preferred_element_type=F32)
    kn = jnp.dot(ckvn, wk_ref[...], preferred_element_type=F32)
    vt = lax.dot_general(wvt_ref[...], ckvn, _NT, preferred_element_type=F32)
    naq = proj(_C_NAQ, _C_NAK) * (HEAD_DIM ** -0.5 * LOG2_E)
    scale = MLA_QK_DIM ** -0.5 * LOG2_E
    ones = jnp.ones((V_ROWS - HEAD_DIM, vt.shape[1]), BF16)
    lane_half = lax.broadcasted_iota(jnp.int32, (1, V7X_LANES), 1) // HEAD_DIM
    for hd in range(HEADS):
        lanes = slice(hd * V7X_LANES, (hd + 1) * V7X_LANES)
        q_ref[hd] = (_rope_lanes(q[:, lanes], cos, sin_lo, sin_hi) * scale).astype(BF16)
        pair = kn[:, (hd // 2) * V7X_LANES:(hd // 2 + 1) * V7X_LANES]
        nope = pair if hd % 2 == 0 else pltpu.roll(pair, HEAD_DIM, 1)
        k_ref[hd] = (jnp.where(lane_half == 0, nope, 0.0) + k_rope).astype(BF16)
        vt_ref[hd, 0, :HEAD_DIM, :] = vt[hd * HEAD_DIM:(hd + 1) * HEAD_DIM, :].astype(BF16)
        vt_ref[hd, 0, HEAD_DIM:, :] = ones
        na_pair = naq[:, (hd // 2) * V7X_LANES:(hd // 2 + 1) * V7X_LANES]
        naq_ref[hd] = jnp.where(lane_half == hd % 2, na_pair, 0.0).astype(BF16)
    nak = proj(_C_NAK, _C_END).astype(BF16)
    for pr in range(HEADS // 2):
        nak_ref[pr] = nak[:, pr * V7X_LANES:(pr + 1) * V7X_LANES]
    navt = lax.dot_general(wnavt_ref[...], h, _NT, preferred_element_type=F32).astype(BF16)
    for hd in range(HEADS):
        navt_ref[hd * V_ROWS:hd * V_ROWS + HEAD_DIM, :] = navt[hd * HEAD_DIM:(hd + 1) * HEAD_DIM, :]
        navt_ref[hd * V_ROWS + HEAD_DIM:(hd + 1) * V_ROWS, :] = ones


def _proj_call(x2d, seq, w):
    n = x2d.shape[0]
    tm = TOKEN_TILE
    n_tiles = n // tm
    pos_tiles = seq // tm
    const = lambda i: (0, 0)
    single = dict(pipeline_mode=pl.Buffered(1))
    in_specs = [
        pl.BlockSpec((tm, D_MODEL), lambda i: (i, 0)),
        pl.BlockSpec((1, D_MODEL), const),
        pl.BlockSpec((D_MODEL, _C_END), const, **single),
        pl.BlockSpec((1, Q_LORA_RANK), const),
        pl.BlockSpec((1, KV_LORA_RANK), const),
        pl.BlockSpec((Q_LORA_RANK, HEADS * V7X_LANES), const, **single),
        pl.BlockSpec((KV_LORA_RANK, MLA_WIDTH), const, **single),
        pl.BlockSpec((MLA_WIDTH, KV_LORA_RANK), const, **single),
        pl.BlockSpec((NA_WIDTH, D_MODEL), const, **single),
        pl.BlockSpec((tm, V7X_LANES), lambda i: (i % pos_tiles, 0)),
        pl.BlockSpec((tm, V7X_LANES), lambda i: (i % pos_tiles, 0)),
        pl.BlockSpec((tm, V7X_LANES), lambda i: (i % pos_tiles, 0)),
    ]
    out_shape = (
        jax.ShapeDtypeStruct((HEADS, n, V7X_LANES), BF16),
        jax.ShapeDtypeStruct((HEADS, n, V7X_LANES), BF16),
        jax.ShapeDtypeStruct((HEADS, n_tiles, V_ROWS, tm), BF16),
        jax.ShapeDtypeStruct((HEADS, n, V7X_LANES), BF16),
        jax.ShapeDtypeStruct((HEADS // 2, n, V7X_LANES), BF16),
        jax.ShapeDtypeStruct((HEADS * V_ROWS, n), BF16),
    )
    out_specs = (
        pl.BlockSpec((HEADS, tm, V7X_LANES), lambda i: (0, i, 0)),
        pl.BlockSpec((HEADS, tm, V7X_LANES), lambda i: (0, i, 0)),
        pl.BlockSpec((HEADS, 1, V_ROWS, tm), lambda i: (0, i, 0, 0)),
        pl.BlockSpec((HEADS, tm, V7X_LANES), lambda i: (0, i, 0)),
        pl.BlockSpec((HEADS // 2, tm, V7X_LANES), lambda i: (0, i, 0)),
        pl.BlockSpec((HEADS * V_ROWS, tm), lambda i: (0, i)),
    )
    operands = (x2d, w["attn_g"], w["w_main"], w["q_g"], w["kv_g"], w["w_uq"], w["w_k"], w["w_vt"],
                w["w_navt"], w["cos"], w["sin_lo"], w["sin_hi"])
    return _launch(_proj_kernel, "proj", (n_tiles,), operands, in_specs, out_specs, out_shape)


def _mla_kernel(q_ref, k_ref, vt_ref, o_ref, acc_ref, s_ref, *, n_q, n_sc, n_u, group):
    tq, tk = MLA_Q_TILE, TOKEN_TILE
    acc_ref[...] = jnp.zeros_like(acc_ref)

    def split(u):
        return (u // n_sc, u % n_sc) if n_sc > 1 else (u, 0)

    def produce(u, qi):
        hd, sc = split(u)
        q = q_ref[hd, qi * tq:(qi + 1) * tq, :]
        col_max = []
        for c in range(group):
            k0 = pl.multiple_of((sc * group + c) * tk, tk)
            s = lax.dot_general(k_ref[hd, pl.ds(k0, tk), :], q, _NT, preferred_element_type=F32)
            s_ref[qi % 2, c] = s
            col_max.append(jnp.max(s, axis=0, keepdims=True))
        return functools.reduce(jnp.maximum, col_max)

    def consume(u, qi, m, cmax):
        hd, sc = split(u)
        m = jnp.where(sc == 0, NEG_INF, m)
        m_new = jnp.maximum(m, cmax)
        alpha = jnp.exp2(m - m_new)
        pv = 0.0
        for c in range(group):
            p = jnp.exp2(s_ref[qi % 2, c] - m_new).astype(BF16)
            pv = pv + jnp.dot(vt_ref[hd, sc * group + c], p, preferred_element_type=F32)
        acc = alpha * acc_ref[qi] + pv
        acc_ref[qi] = acc
        o_ref[_rows(hd * HEAD_DIM, HEAD_DIM), qi * tq:(qi + 1) * tq] = (
            acc[:HEAD_DIM] / acc[HEAD_DIM:HEAD_DIM + 1])
        return m_new

    def sweep(u, ms, cmax, last):
        new_ms = []
        for qi in range(n_q):
            if qi + 1 < n_q:
                cmax_next = produce(u, qi + 1)
            else:
                cmax_next = None if last else produce(u + 1, 0)
            new_ms.append(consume(u, qi, ms[qi], cmax))
            cmax = cmax_next
        return tuple(new_ms), cmax

    ms = tuple(jnp.full((1, tq), NEG_INF, F32) for _ in range(n_q))
    carry = lax.fori_loop(0, n_u - 1, lambda u, c: sweep(u, *c, False), (ms, produce(0, 0)))
    sweep(n_u - 1, *carry, True)


def _mla_call(q, k, vt, batch, seq):
    n = q.shape[1]
    qb = min(seq, MLA_Q_BLOCK)
    n_qb = seq // qb
    n_kv = seq // TOKEN_TILE
    n_q = qb // MLA_Q_TILE
    group = math.gcd(n_kv, MLA_KV_GROUP)
    n_sc = n_kv // group
    hb = max(1, min(HEADS, MLA_MIN_SWEEPS // n_sc))
    assert n_q % 2 == 0 and HEADS % hb == 0
    kernel = functools.partial(_mla_kernel, n_q=n_q, n_sc=n_sc, n_u=hb * n_sc, group=group)
    in_specs = [
        pl.BlockSpec((hb, qb, V7X_LANES), lambda b, h, i: (h, b * n_qb + i, 0)),
        pl.BlockSpec((hb, seq, V7X_LANES), lambda b, h, i: (h, b, 0)),
        pl.BlockSpec((hb, n_kv, V_ROWS, TOKEN_TILE), lambda b, h, i: (h, b, 0, 0)),
    ]
    return _launch(
        kernel, "mla", (batch, HEADS // hb, n_qb), (q, k, vt), in_specs,
        pl.BlockSpec((hb * HEAD_DIM, qb), lambda b, h, i: (h, b * n_qb + i)),
        jax.ShapeDtypeStruct((MLA_WIDTH, n), F32),
        scratch_shapes=[pltpu.VMEM((n_q, V_ROWS, MLA_Q_TILE), F32),
                        pltpu.VMEM((2, group, TOKEN_TILE, MLA_Q_TILE), F32)])


def _na_kernel(q_ref, *refs, groups, ng):
    n_win = ng + NA_CHUNKS - 1
    k_refs, v_refs = refs[:n_win], refs[n_win:2 * n_win]
    bm_ref, o_ref, s_ref = refs[2 * n_win:]
    first_group = ng * pl.program_id(1)
    kinds = [jnp.where(first_group + e == 0, 0, jnp.where(first_group + e == groups - 1, 2, 1))
             for e in range(ng)]

    def produce(hd, e):
        q = q_ref[hd, e * NA_TOK:(e + 1) * NA_TOK, :]
        col_max = []
        for t in range(NA_CHUNKS):
            s = (lax.dot_general(k_refs[e + t][hd // 2], q, _NT, preferred_element_type=F32)
                 + bm_ref[kinds[e], hd, t])
            s_ref[e % 2, t] = s
            col_max.append(jnp.max(s, axis=0, keepdims=True))
        return functools.reduce(jnp.maximum, col_max)

    def consume(hd, e, m):
        v_rows = _rows(hd * V_ROWS, V_ROWS)
        o = 0.0
        for t in range(NA_CHUNKS):
            p = jnp.exp2(s_ref[e % 2, t] - m).astype(BF16)
            o = o + jnp.dot(v_refs[e + t][v_rows, :], p, preferred_element_type=F32)
        o_ref[_rows(hd * HEAD_DIM, HEAD_DIM), e * NA_TOK:(e + 1) * NA_TOK] = (
            o[:HEAD_DIM] / o[HEAD_DIM:HEAD_DIM + 1])

    def head_pair(hd, m, last):
        items = [(h, e) for h in (hd, hd + 1) for e in range(ng)]
        for n, item in enumerate(items):
            if n + 1 < len(items):
                m_next = produce(*items[n + 1])
            else:
                m_next = None if last else produce(hd + 2, 0)
            consume(*item, m)
            m = m_next
        return m

    m = lax.fori_loop(0, HEADS // 2 - 1, lambda i, m: head_pair(2 * i, m, False), produce(0, 0))
    head_pair(HEADS - 2, m, True)


def _na_call(naq, nak, navt, bias_mask, batch, seq):
    n = nak.shape[1]
    groups = seq // NA_TOK
    ng = math.gcd(groups, NA_GROUPS)
    steps = groups // ng
    assert ng % 2 == 0

    def chunk(b, i, t):
        return b * groups + jnp.clip(ng * i - 1 + t, 0, groups - 1)

    n_win = ng + NA_CHUNKS - 1
    k_specs = [pl.BlockSpec((HEADS // 2, NA_TOK, V7X_LANES),
                            functools.partial(lambda b, i, t: (0, chunk(b, i, t), 0), t=t))
               for t in range(n_win)]
    v_specs = [pl.BlockSpec((HEADS * V_ROWS, NA_TOK),
                            functools.partial(lambda b, i, t: (0, chunk(b, i, t)), t=t))
               for t in range(n_win)]
    in_specs = ([pl.BlockSpec((HEADS, ng * NA_TOK, V7X_LANES),
                              lambda b, i: (0, b * steps + i, 0))]
                + k_specs + v_specs
                + [pl.BlockSpec((3, HEADS, NA_CHUNKS, NA_TOK, NA_TOK), lambda b, i: (0, 0, 0, 0, 0),
                                pipeline_mode=pl.Buffered(1))])
    operands = (naq, *([nak] * n_win), *([navt] * n_win), bias_mask)
    return _launch(
        functools.partial(_na_kernel, groups=groups, ng=ng), "na", (batch, steps), operands, in_specs,
        pl.BlockSpec((NA_WIDTH, ng * NA_TOK), lambda b, i: (0, b * steps + i)),
        jax.ShapeDtypeStruct((NA_WIDTH, n), F32),
        scratch_shapes=[pltpu.VMEM((2, NA_CHUNKS, NA_TOK, NA_TOK), F32)])


def _post_kernel(at_ref, bt_ref, x_ref, ga_ref, gb_ref, wo_ref, gf_ref, wg_ref, wu_ref, wd_ref,
                 gl_ref, o_ref, acc_ref, gu_ref):
    mix_t = jnp.concatenate([_rms_cols(at_ref[...], ga_ref[...]),
                             _rms_cols(bt_ref[...], gb_ref[...])], axis=0).astype(BF16)
    x1 = x_ref[...] + lax.dot_general(mix_t, wo_ref[...], _TN, preferred_element_type=F32)
    h2 = _rms_rows(x1, gf_ref[...]).astype(BF16)
    acc_ref[...] = x1

    def produce(c, slot):
        cols = slice(c * FF_CHUNK, (c + 1) * FF_CHUNK)
        gu_ref[slot, 0] = jnp.dot(h2, wg_ref[:, cols], preferred_element_type=F32)
        gu_ref[slot, 1] = jnp.dot(h2, wu_ref[:, cols], preferred_element_type=F32)

    def consume(c, slot):
        gate = gu_ref[slot, 0]
        act = (gate * jax.nn.sigmoid(gate) * gu_ref[slot, 1]).astype(BF16)
        acc_ref[...] += jnp.dot(act, wd_ref[c * FF_CHUNK:(c + 1) * FF_CHUNK, :],
                                preferred_element_type=F32)

    n_ff = D_FF // FF_CHUNK
    produce(0, 0)
    for c in range(n_ff):
        if c + 1 < n_ff:
            produce(c + 1, (c + 1) % 2)
        consume(c, c % 2)
    o_ref[...] = _rms_rows(acc_ref[...], gl_ref[...])


def _post_call(at, bt, x2d, w):
    n = x2d.shape[0]
    tm = TOKEN_TILE
    const2 = lambda i: (0, 0)
    single = dict(pipeline_mode=pl.Buffered(1))
    in_specs = [
        pl.BlockSpec((MLA_WIDTH, tm), lambda i: (0, i)),
        pl.BlockSpec((NA_WIDTH, tm), lambda i: (0, i)),
        pl.BlockSpec((tm, D_MODEL), lambda i: (i, 0)),
        pl.BlockSpec((MLA_WIDTH, 1), const2),
        pl.BlockSpec((NA_WIDTH, 1), const2),
        pl.BlockSpec((D_MODEL, D_MODEL), const2, **single),
        pl.BlockSpec((1, D_MODEL), const2),
        pl.BlockSpec((D_MODEL, D_FF), const2, **single),
        pl.BlockSpec((D_MODEL, D_FF), const2, **single),
        pl.BlockSpec((D_FF, D_MODEL), const2, **single),
        pl.BlockSpec((1, D_MODEL), const2),
    ]
    operands = (at, bt, x2d, w["mla_out_g"], w["na_out_g"], w["w_o"], w["ffn_g"], w["w_gate"],
                w["w_up"], w["w_down"], w["final_g"])
    return _launch(
        _post_kernel, "post", (n // tm,), operands, in_specs,
        pl.BlockSpec((tm, D_MODEL), lambda i: (i, 0)), jax.ShapeDtypeStruct((n, D_MODEL), F32),
        scratch_shapes=[pltpu.VMEM((tm, D_MODEL), F32), pltpu.VMEM((2, 2, tm, FF_CHUNK), F32)])


def _pad_heads(wm, width):
    r = wm.shape[0]
    wm = wm.reshape(r, HEADS, width)
    return jnp.pad(wm, ((0, 0), (0, 0), (0, V7X_LANES - width))).reshape(r, HEADS * V7X_LANES)


def _prep_layer(attn_norm_g, w_in, q_norm_g, kv_norm_g, w_uq, w_ukv, na_rpb, mla_out_g, na_out_g,
                w_o, ffn_norm_g, w_gate, w_up, w_down, final_norm_g):
    o_kr = Q_LORA_RANK + KV_LORA_RANK
    o_na = o_kr + MLA_ROPE_DIM
    w_kr = w_in[:, o_kr:o_na]
    kr_block = jnp.zeros((D_MODEL, V7X_LANES), F32).at[:, HEAD_DIM:HEAD_DIM + MLA_ROPE_DIM].set(w_kr)
    w_nav = w_in[:, o_na + 2 * NA_WIDTH:]
    w_main = jnp.concatenate([w_in[:, :o_kr], kr_block, w_in[:, o_na:o_na + 2 * NA_WIDTH]],
                             axis=1).astype(BF16)

    w_ukv_h = w_ukv.reshape(KV_LORA_RANK, HEADS, 2 * HEAD_DIM)
    w_k = w_ukv_h[:, :, :HEAD_DIM].reshape(KV_LORA_RANK, MLA_WIDTH)
    w_v = w_ukv_h[:, :, HEAD_DIM:].reshape(KV_LORA_RANK, MLA_WIDTH)
    return {
        "attn_g": attn_norm_g.reshape(1, D_MODEL),
        "w_main": w_main,
        "q_g": q_norm_g.reshape(1, Q_LORA_RANK),
        "kv_g": kv_norm_g.reshape(1, KV_LORA_RANK),
        "w_uq": _pad_heads(w_uq, MLA_QK_DIM).astype(BF16),
        "w_k": w_k.astype(BF16),
        "w_vt": w_v.T.astype(BF16),
        "w_navt": w_nav.T.astype(BF16),
        "rpb": na_rpb,
        "mla_out_g": mla_out_g.reshape(MLA_WIDTH, 1),
        "na_out_g": na_out_g.reshape(NA_WIDTH, 1),
        "w_o": w_o.astype(BF16),
        "ffn_g": ffn_norm_g.reshape(1, D_MODEL),
        "w_gate": w_gate.astype(BF16),
        "w_up": w_up.astype(BF16),
        "w_down": w_down.astype(BF16),
        "final_g": final_norm_g.reshape(1, D_MODEL),
    }


def _rope_tables(seq):
    half = MLA_ROPE_DIM // 2
    inv = ROPE_THETA ** (-jnp.arange(0, MLA_ROPE_DIM, 2, dtype=F32) / MLA_ROPE_DIM)
    ang = jnp.arange(seq, dtype=F32)[:, None] * inv[None, :]
    cos, sin = jnp.cos(ang), jnp.sin(ang)
    zeros = jnp.zeros((seq, half), F32)
    ones = jnp.ones((seq, HEAD_DIM), F32)
    tail = jnp.zeros((seq, V7X_LANES - MLA_QK_DIM), F32)
    cos_t = jnp.concatenate([ones, cos, cos, tail], axis=1)
    lead = jnp.zeros((seq, HEAD_DIM), F32)
    sin_lo = jnp.concatenate([lead, -sin, zeros, tail], axis=1)
    sin_hi = jnp.concatenate([lead, zeros, sin, tail], axis=1)
    return cos_t, sin_lo, sin_hi


def _na_bias_mask(rpb):
    qc = np.arange(GRID_W)[None, :]
    kc = np.arange(GRID_W)[:, None]
    col_start = np.clip(qc - NA_KW // 2, 0, GRID_W - NA_KW)
    col_ok = (kc >= col_start) & (kc < col_start + NA_KW)
    period = 2 * GRID_W
    n_d = 2 * NA_KH - 1
    rpb = rpb * LOG2_E
    row = jnp.concatenate([rpb[..., NA_KW - 1::-1],
                           jnp.zeros((HEADS, n_d, period - (2 * NA_KW - 1)), F32),
                           rpb[..., :NA_KW - 1:-1]], axis=-1)
    skew = jnp.broadcast_to(row[:, :, None, :], (HEADS, n_d, GRID_W, period))
    skew = skew.reshape(HEADS, n_d, GRID_W * period)[..., :GRID_W * (period - 1)]
    toe = skew.reshape(HEADS, n_d, GRID_W, period - 1)[..., :GRID_W]

    i = np.arange(NA_CHUNKS * NA_ROWS)[:, None]
    j = np.arange(NA_ROWS)[None, :]
    di = np.clip(i - NA_ROWS - j + NA_KH - 1, 0, n_d - 1)
    row_ok = np.stack([(i >= NA_ROWS) & (j >= 0),
                       (i - j >= 0) & (i - j < NA_KH),
                       (i < NA_KH) & (j >= 0)])
    ok = row_ok[:, :, None, :, None] & col_ok[None, None, :, None, :]
    bias = toe[:, di].transpose(0, 1, 3, 2, 4)
    bm = jnp.where(ok[:, None], bias[None], NEG_INF)
    return bm.reshape(3, HEADS, NA_CHUNKS, NA_TOK, NA_TOK)


def _run_trunk(x, w, bias_mask):
    batch, seq, _ = x.shape
    assert seq % TOKEN_TILE == 0 and seq % (2 * NA_TOK) == 0
    assert seq % min(seq, MLA_Q_BLOCK) == 0
    x2d = x.reshape(batch * seq, D_MODEL)
    cos_t, sin_lo, sin_hi = _rope_tables(seq)
    wt = dict(w, cos=cos_t, sin_lo=sin_lo, sin_hi=sin_hi)
    q, k, vt, naq, nak, navt = _proj_call(x2d, seq, wt)
    at = _mla_call(q, k, vt, batch, seq)
    bt = _na_call(naq, nak, navt, bias_mask, batch, seq)
    y = _post_call(at, bt, x2d, w)
    return y.reshape(batch, seq, D_MODEL)


def kernel(x_prompt, x_sample, attn_norm_g, w_in, q_norm_g, kv_norm_g, w_uq, w_ukv, na_rpb,
           mla_out_g, na_out_g, w_o, ffn_norm_g, w_gate, w_up, w_down, final_norm_g):
    assert attn_norm_g.shape[0] == 1, "single-layer trunk"
    w = _prep_layer(attn_norm_g[0], w_in[0], q_norm_g[0], kv_norm_g[0], w_uq[0], w_ukv[0], na_rpb[0],
                    mla_out_g[0], na_out_g[0], w_o[0], ffn_norm_g[0], w_gate[0], w_up[0], w_down[0],
                    final_norm_g)
    bias_mask = _na_bias_mask(w["rpb"])
    return (_run_trunk(x_prompt, w, bias_mask), _run_trunk(x_sample, w, bias_mask))
```

```python
import functools
import math

import jax
import jax.numpy as jnp
import numpy as np
from jax import lax
from jax.experimental import pallas as pl
from jax.experimental.pallas import tpu as pltpu

D_MODEL = 1024
MLA_WIDTH = D_MODEL // 2
NA_WIDTH = D_MODEL - MLA_WIDTH
HEADS = 8
HEAD_DIM = 64
MLA_ROPE_DIM = 32
MLA_QK_DIM = HEAD_DIM + MLA_ROPE_DIM
Q_LORA_RANK = 384
KV_LORA_RANK = 256
ROPE_THETA = 10000.0
GRID_W = 64
NA_KH = 8
NA_KW = 16
D_FF = int(math.ceil(8 * D_MODEL / 3 / 256)) * 256
EPS = 1e-6
NEG_INF = -1e30
LOG2_E = math.log2(math.e)

V7X_LANES = 128
V7X_BF16_SUBLANES = 16
V7X_VMEM_BYTES = 64 * 1024 * 1024
VMEM_SPILL_ALLOWANCE = 8 * 1024 * 1024

V_ROWS = HEAD_DIM + V7X_BF16_SUBLANES

TOKEN_TILE = 512
MLA_Q_TILE = 256
MLA_Q_BLOCK = 4096
MLA_KV_GROUP = 4
MLA_MIN_SWEEPS = 8
NA_ROWS = 4
NA_TOK = NA_ROWS * GRID_W
NA_CHUNKS = 3
NA_GROUPS = 8
NA_KINDS = 3
FF_CHUNK = 256

_C_Q = 0
_C_KV = _C_Q + Q_LORA_RANK
_C_KR = _C_KV + KV_LORA_RANK
_C_NAQ = _C_KR + V7X_LANES
_C_NAK = _C_NAQ + NA_WIDTH
_C_END = _C_NAK + NA_WIDTH

_NT = (((1,), (1,)), ((), ()))
_TN = (((0,), (0,)), ((), ()))
BF16 = jnp.bfloat16
F32 = jnp.float32


def _rms_rows(x, g):
    return x * lax.rsqrt(jnp.mean(x * x, axis=-1, keepdims=True) + EPS) * g


def _rms_cols(xt, g):
    return xt * lax.rsqrt(jnp.mean(xt * xt, axis=0, keepdims=True) + EPS) * g


def _rope_lanes(x, cos, sin_lo, sin_hi):
    half = MLA_ROPE_DIM // 2
    return (x * cos + pltpu.roll(x, V7X_LANES - half, 1) * sin_lo
            + pltpu.roll(x, half, 1) * sin_hi)


def _rows(start, size):
    return pl.ds(start if isinstance(start, int) else pl.multiple_of(start, size), size)


def _launch(body, name, grid, operands, in_specs, out_specs, out_shape, scratch_shapes=()):
    def window(spec, dtype):
        buffers = 1 if spec.pipeline_mode is not None else 2
        return buffers * math.prod(spec.block_shape) * jnp.dtype(dtype).itemsize

    multi = isinstance(out_shape, (tuple, list))
    outs = tuple(out_shape) if multi else (out_shape,)
    o_specs = tuple(out_specs) if multi else (out_specs,)
    vmem = (sum(window(s, a.dtype) for s, a in zip(in_specs, operands))
            + sum(window(s, o.dtype) for s, o in zip(o_specs, outs))
            + sum(math.prod(s.shape) * jnp.dtype(s.dtype).itemsize for s in scratch_shapes)
            + VMEM_SPILL_ALLOWANCE)
    assert vmem <= V7X_VMEM_BYTES, (name, vmem)
    return pl.pallas_call(
        body, grid=grid, in_specs=list(in_specs), out_specs=out_specs, out_shape=out_shape,
        scratch_shapes=list(scratch_shapes),
        compiler_params=pltpu.CompilerParams(
            dimension_semantics=("arbitrary",) * len(grid), vmem_limit_bytes=vmem),
        name=name,
    )(*operands)


def _proj_kernel(x_ref, ga_ref, wmain_ref, gq_ref, gkv_ref, wuq_ref, wk_ref, wvt_ref, wnavt_ref,
                 cos_ref, sinlo_ref, sinhi_ref,
                 q_ref, k_ref, vt_ref, naq_ref, nak_ref, navt_ref):
    h = _rms_rows(x_ref[...], ga_ref[...]).astype(BF16)

    def proj(lo, hi):
        return jnp.dot(h, wmain_ref[:, lo:hi], preferred_element_type=F32)

    cqn = _rms_rows(proj(_C_Q, _C_KV), gq_ref[...]).astype(BF16)
    ckvn = _rms_rows(proj(_C_KV, _C_KR), gkv_ref[...]).astype(BF16)
    cos, sin_lo, sin_hi = cos_ref[...], sinlo_ref[...], sinhi_ref[...]
    k_rope = _rope_lanes(proj(_C_KR, _C_NAQ), cos, sin_lo, sin_hi)

    q = jnp.dot(cqn, wuq_ref[...], preferred_element_type=F32)
    kn = jnp.dot(ckvn, wk_ref[...], preferred_element_type=F32)
    vt = lax.dot_general(wvt_ref[...], ckvn, _NT, preferred_element_type=F32)
    naq = proj(_C_NAQ, _C_NAK) * (HEAD_DIM ** -0.5 * LOG2_E)
    scale = MLA_QK_DIM ** -0.5 * LOG2_E
    ones = jnp.ones((V_ROWS - HEAD_DIM, vt.shape[1]), BF16)
    lane_half = lax.broadcasted_iota(jnp.int32, (1, V7X_LANES), 1) // HEAD_DIM
    for hd in range(HEADS):
        lanes = slice(hd * V7X_LANES, (hd + 1) * V7X_LANES)
        q_ref[hd] = (_rope_lanes(q[:, lanes], cos, sin_lo, sin_hi) * scale).astype(BF16)
        pair = kn[:, (hd // 2) * V7X_LANES:(hd // 2 + 1) * V7X_LANES]
        nope = pair if hd % 2 == 0 else pltpu.roll(pair, HEAD_DIM, 1)
        k_ref[hd] = (jnp.where(lane_half == 0, nope, 0.0) + k_rope).astype(BF16)
        vt_ref[hd, 0, :HEAD_DIM, :] = vt[hd * HEAD_DIM:(hd + 1) * HEAD_DIM, :].astype(BF16)
        vt_ref[hd, 0, HEAD_DIM:, :] = ones
        na_pair = naq[:, (hd // 2) * V7X_LANES:(hd // 2 + 1) * V7X_LANES]
        naq_ref[hd] = jnp.where(lane_half == hd % 2, na_pair, 0.0).astype(BF16)
    nak = proj(_C_NAK, _C_END).astype(BF16)
    for pr in range(HEADS // 2):
        nak_ref[pr] = nak[:, pr * V7X_LANES:(pr + 1) * V7X_LANES]
    navt = lax.dot_general(wnavt_ref[...], h, _NT, preferred_element_type=F32).astype(BF16)
    for hd in range(HEADS):
        navt_ref[hd * V_ROWS:hd * V_ROWS + HEAD_DIM, :] = navt[hd * HEAD_DIM:(hd + 1) * HEAD_DIM, :]
        navt_ref[hd * V_ROWS + HEAD_DIM:(hd + 1) * V_ROWS, :] = ones


def _proj_call(x2d, seq, w):
    n = x2d.shape[0]
    tm = TOKEN_TILE
    n_tiles = n // tm
    pos_tiles = seq // tm
    const = lambda i: (0, 0)
    single = dict(pipeline_mode=pl.Buffered(1))
    in_specs = [
        pl.BlockSpec((tm, D_MODEL), lambda i: (i, 0)),
        pl.BlockSpec((1, D_MODEL), const),
        pl.BlockSpec((D_MODEL, _C_END), const, **single),
        pl.BlockSpec((1, Q_LORA_RANK), const),
        pl.BlockSpec((1, KV_LORA_RANK), const),
        pl.BlockSpec((Q_LORA_RANK, HEADS * V7X_LANES), const, **single),
        pl.BlockSpec((KV_LORA_RANK, MLA_WIDTH), const, **single),
        pl.BlockSpec((MLA_WIDTH, KV_LORA_RANK), const, **single),
        pl.BlockSpec((NA_WIDTH, D_MODEL), const, **single),
        pl.BlockSpec((tm, V7X_LANES), lambda i: (i % pos_tiles, 0)),
        pl.BlockSpec((tm, V7X_LANES), lambda i: (i % pos_tiles, 0)),
        pl.BlockSpec((tm, V7X_LANES), lambda i: (i % pos_tiles, 0)),
    ]
    out_shape = (
        jax.ShapeDtypeStruct((HEADS, n, V7X_LANES), BF16),
        jax.ShapeDtypeStruct((HEADS, n, V7X_LANES), BF16),
        jax.ShapeDtypeStruct((HEADS, n_tiles, V_ROWS, tm), BF16),
        jax.ShapeDtypeStruct((HEADS, n, V7X_LANES), BF16),
        jax.ShapeDtypeStruct((HEADS // 2, n, V7X_LANES), BF16),
        jax.ShapeDtypeStruct((HEADS * V_ROWS, n), BF16),
    )
    out_specs = (
        pl.BlockSpec((HEADS, tm, V7X_LANES), lambda i: (0, i, 0)),
        pl.BlockSpec((HEADS, tm, V7X_LANES), lambda i: (0, i, 0)),
        pl.BlockSpec((HEADS, 1, V_ROWS, tm), lambda i: (0, i, 0, 0)),
        pl.BlockSpec((HEADS, tm, V7X_LANES), lambda i: (0, i, 0)),
        pl.BlockSpec((HEADS // 2, tm, V7X_LANES), lambda i: (0, i, 0)),
        pl.BlockSpec((HEADS * V_ROWS, tm), lambda i: (0, i)),
    )
    operands = (x2d, w["attn_g"], w["w_main"], w["q_g"], w["kv_g"], w["w_uq"], w["w_k"], w["w_vt"],
                w["w_navt"], w["cos"], w["sin_lo"], w["sin_hi"])
    return _launch(_proj_kernel, "proj", (n_tiles,), operands, in_specs, out_specs, out_shape)


def _mla_kernel(q_ref, k_ref, vt_ref, o_ref, acc_ref, s_ref, *, n_q, n_sc, n_u, group):
    tq, tk = MLA_Q_TILE, TOKEN_TILE
    acc_ref[...] = jnp.zeros_like(acc_ref)

    def split(u):
        return (u // n_sc, u % n_sc) if n_sc > 1 else (u, 0)

    def produce(u, qi):
        hd, sc = split(u)
        q = q_ref[hd, qi * tq:(qi + 1) * tq, :]
        col_max = []
        for c in range(group):
            k0 = pl.multiple_of((sc * group + c) * tk, tk)
            s = lax.dot_general(k_ref[hd, pl.ds(k0, tk), :], q, _NT, preferred_element_type=F32)
            s_ref[qi % 2, c] = s
            col_max.append(jnp.max(s, axis=0, keepdims=True))
        return functools.reduce(jnp.maximum, col_max)

    def consume(u, qi, m, cmax):
        hd, sc = split(u)
        m = jnp.where(sc == 0, NEG_INF, m)
        m_new = jnp.maximum(m, cmax)
        alpha = jnp.exp2(m - m_new)
        pv = 0.0
        for c in range(group):
            p = jnp.exp2(s_ref[qi % 2, c] - m_new).astype(BF16)
            pv = pv + jnp.dot(vt_ref[hd, sc * group + c], p, preferred_element_type=F32)
        acc = alpha * acc_ref[qi] + pv
        acc_ref[qi] = acc
        o_ref[_rows(hd * HEAD_DIM, HEAD_DIM), qi * tq:(qi + 1) * tq] = (
            acc[:HEAD_DIM] / acc[HEAD_DIM:HEAD_DIM + 1])
        return m_new

    def sweep(u, ms, cmax, last):
        new_ms = []
        for qi in range(n_q):
            if qi + 1 < n_q:
                cmax_next = produce(u, qi + 1)
            else:
                cmax_next = None if last else produce(u + 1, 0)
            new_ms.append(consume(u, qi, ms[qi], cmax))
            cmax = cmax_next
        return tuple(new_ms), cmax

    ms = tuple(jnp.full((1, tq), NEG_INF, F32) for _ in range(n_q))
    carry = lax.fori_loop(0, n_u - 1, lambda u, c: sweep(u, *c, False), (ms, produce(0, 0)))
    sweep(n_u - 1, *carry, True)


def _mla_call(q, k, vt, batch, seq):
    n = q.shape[1]
    qb = min(seq, MLA_Q_BLOCK)
    n_qb = seq // qb
    n_kv = seq // TOKEN_TILE
    n_q = qb // MLA_Q_TILE
    group = math.gcd(n_kv, MLA_KV_GROUP)
    n_sc = n_kv // group
    hb = max(1, min(HEADS, MLA_MIN_SWEEPS // n_sc))
    assert n_q % 2 == 0 and HEADS % hb == 0
    kernel = functools.partial(_mla_kernel, n_q=n_q, n_sc=n_sc, n_u=hb * n_sc, group=group)
    in_specs = [
        pl.BlockSpec((hb, qb, V7X_LANES), lambda b, h, i: (h, b * n_qb + i, 0)),
        pl.BlockSpec((hb, seq, V7X_LANES), lambda b, h, i: (h, b, 0)),
        pl.BlockSpec((hb, n_kv, V_ROWS, TOKEN_TILE), lambda b, h, i: (h, b, 0, 0)),
    ]
    return _launch(
        kernel, "mla", (batch, HEADS // hb, n_qb), (q, k, vt), in_specs,
        pl.BlockSpec((hb * HEAD_DIM, qb), lambda b, h, i: (h, b * n_qb + i)),
        jax.ShapeDtypeStruct((MLA_WIDTH, n), F32),
        scratch_shapes=[pltpu.VMEM((n_q, V_ROWS, MLA_Q_TILE), F32),
                        pltpu.VMEM((2, group, TOKEN_TILE, MLA_Q_TILE), F32)])


def _na_kernel(q_ref, *refs, groups, ng):
    n_win = ng + NA_CHUNKS - 1
    k_refs, v_refs = refs[:n_win], refs[n_win:2 * n_win]
    bm_ref, o_ref, s_ref = refs[2 * n_win:]
    first_group = ng * pl.program_id(1)
    kinds = [jnp.where(first_group + e == 0, 0, jnp.where(first_group + e == groups - 1, 2, 1))
             for e in range(ng)]

    def produce(hd, e):
        q = q_ref[hd, e * NA_TOK:(e + 1) * NA_TOK, :]
        col_max = []
        for t in range(NA_CHUNKS):
            s = (lax.dot_general(k_refs[e + t][hd // 2], q, _NT, preferred_element_type=F32)
                 + bm_ref[kinds[e], hd, t])
            s_ref[e % 2, t] = s
            col_max.append(jnp.max(s, axis=0, keepdims=True))
        return functools.reduce(jnp.maximum, col_max)

    def consume(hd, e, m):
        v_rows = _rows(hd * V_ROWS, V_ROWS)
        o = 0.0
        for t in range(NA_CHUNKS):
            p = jnp.exp2(s_ref[e % 2, t] - m).astype(BF16)
            o = o + jnp.dot(v_refs[e + t][v_rows, :], p, preferred_element_type=F32)
        o_ref[_rows(hd * HEAD_DIM, HEAD_DIM), e * NA_TOK:(e + 1) * NA_TOK] = (
            o[:HEAD_DIM] / o[HEAD_DIM:HEAD_DIM + 1])

    def head_pair(hd, m, last):
        items = [(h, e) for h in (hd, hd + 1) for e in range(ng)]
        for n, item in enumerate(items):
            if n + 1 < len(items):
                m_next = produce(*items[n + 1])
            else:
                m_next = None if last else produce(hd + 2, 0)
            consume(*item, m)
            m = m_next
        return m

    m = lax.fori_loop(0, HEADS // 2 - 1, lambda i, m: head_pair(2 * i, m, False), produce(0, 0))
    head_pair(HEADS - 2, m, True)


def _na_call(naq, nak, navt, bias_mask, batch, seq):
    n = nak.shape[1]
    groups = seq // NA_TOK
    ng = math.gcd(groups, NA_GROUPS)
    steps = groups // ng
    assert ng % 2 == 0

    def chunk(b, i, t):
        return b * groups + jnp.clip(ng * i - 1 + t, 0, groups - 1)

    n_win = ng + NA_CHUNKS - 1
    k_specs = [pl.BlockSpec((HEADS // 2, NA_TOK, V7X_LANES),
                            functools.partial(lambda b, i, t: (0, chunk(b, i, t), 0), t=t))
               for t in range(n_win)]
    v_specs = [pl.BlockSpec((HEADS * V_ROWS, NA_TOK),
                            functools.partial(lambda b, i, t: (0, chunk(b, i, t)), t=t))
               for t in range(n_win)]
    in_specs = ([pl.BlockSpec((HEADS, ng * NA_TOK, V7X_LANES),
                              lambda b, i: (0, b * steps + i, 0))]
                + k_specs + v_specs
                + [pl.BlockSpec((NA_KINDS, HEADS, NA_CHUNKS, NA_TOK, NA_TOK),
                                lambda b, i: (0, 0, 0, 0, 0),
                                pipeline_mode=pl.Buffered(1))])
    operands = (naq, *([nak] * n_win), *([navt] * n_win), bias_mask)
    return _launch(
        functools.partial(_na_kernel, groups=groups, ng=ng), "na", (batch, steps), operands, in_specs,
        pl.BlockSpec((NA_WIDTH, ng * NA_TOK), lambda b, i: (0, b * steps + i)),
        jax.ShapeDtypeStruct((NA_WIDTH, n), F32),
        scratch_shapes=[pltpu.VMEM((2, NA_CHUNKS, NA_TOK, NA_TOK), F32)])


def _post_kernel(at_ref, bt_ref, x_ref, ga_ref, gb_ref, wo_ref, gf_ref, wg_ref, wu_ref, wd_ref,
                 gl_ref, o_ref, acc_ref, gu_ref):
    mix_t = jnp.concatenate([_rms_cols(at_ref[...], ga_ref[...]),
                             _rms_cols(bt_ref[...], gb_ref[...])], axis=0).astype(BF16)
    x1 = x_ref[...] + lax.dot_general(mix_t, wo_ref[...], _TN, preferred_element_type=F32)
    h2 = _rms_rows(x1, gf_ref[...]).astype(BF16)
    acc_ref[...] = x1

    def produce(c, slot):
        cols = slice(c * FF_CHUNK, (c + 1) * FF_CHUNK)
        gu_ref[slot, 0] = jnp.dot(h2, wg_ref[:, cols], preferred_element_type=F32)
        gu_ref[slot, 1] = jnp.dot(h2, wu_ref[:, cols], preferred_element_type=F32)

    def consume(c, slot):
        gate = gu_ref[slot, 0]
        act = (gate * jax.nn.sigmoid(gate) * gu_ref[slot, 1]).astype(BF16)
        acc_ref[...] += jnp.dot(act, wd_ref[c * FF_CHUNK:(c + 1) * FF_CHUNK, :],
                                preferred_element_type=F32)

    n_ff = D_FF // FF_CHUNK
    produce(0, 0)
    for c in range(n_ff):
        if c + 1 < n_ff:
            produce(c + 1, (c + 1) % 2)
        consume(c, c % 2)
    o_ref[...] = _rms_rows(acc_ref[...], gl_ref[...])


def _post_call(at, bt, x2d, w):
    n = x2d.shape[0]
    tm = TOKEN_TILE
    const2 = lambda i: (0, 0)
    single = dict(pipeline_mode=pl.Buffered(1))
    in_specs = [
        pl.BlockSpec((MLA_WIDTH, tm), lambda i: (0, i)),
        pl.BlockSpec((NA_WIDTH, tm), lambda i: (0, i)),
        pl.BlockSpec((tm, D_MODEL), lambda i: (i, 0)),
        pl.BlockSpec((MLA_WIDTH, 1), const2),
        pl.BlockSpec((NA_WIDTH, 1), const2),
        pl.BlockSpec((D_MODEL, D_MODEL), const2, **single),
        pl.BlockSpec((1, D_MODEL), const2),
        pl.BlockSpec((D_MODEL, D_FF), const2, **single),
        pl.BlockSpec((D_MODEL, D_FF), const2, **single),
        pl.BlockSpec((D_FF, D_MODEL), const2, **single),
        pl.BlockSpec((1, D_MODEL), const2),
    ]
    operands = (at, bt, x2d, w["mla_out_g"], w["na_out_g"], w["w_o"], w["ffn_g"], w["w_gate"],
                w["w_up"], w["w_down"], w["final_g"])
    return _launch(
        _post_kernel, "post", (n // tm,), operands, in_specs,
        pl.BlockSpec((tm, D_MODEL), lambda i: (i, 0)), jax.ShapeDtypeStruct((n, D_MODEL), F32),
        scratch_shapes=[pltpu.VMEM((tm, D_MODEL), F32), pltpu.VMEM((2, 2, tm, FF_CHUNK), F32)])


def _pad_heads(wm, width):
    r = wm.shape[0]
    wm = wm.reshape(r, HEADS, width)
    return jnp.pad(wm, ((0, 0), (0, 0), (0, V7X_LANES - width))).reshape(r, HEADS * V7X_LANES)


def _prep_layer(attn_norm_g, w_in, q_norm_g, kv_norm_g, w_uq, w_ukv, na_rpb, mla_out_g, na_out_g,
                w_o, ffn_norm_g, w_gate, w_up, w_down, final_norm_g):
    o_kr = Q_LORA_RANK + KV_LORA_RANK
    o_na = o_kr + MLA_ROPE_DIM
    w_kr = w_in[:, o_kr:o_na]
    kr_block = jnp.zeros((D_MODEL, V7X_LANES), F32).at[:, HEAD_DIM:HEAD_DIM + MLA_ROPE_DIM].set(w_kr)
    w_nav = w_in[:, o_na + 2 * NA_WIDTH:]
    w_main = jnp.concatenate([w_in[:, :o_kr], kr_block, w_in[:, o_na:o_na + 2 * NA_WIDTH]],
                             axis=1).astype(BF16)

    w_ukv_h = w_ukv.reshape(KV_LORA_RANK, HEADS, 2 * HEAD_DIM)
    w_k = w_ukv_h[:, :, :HEAD_DIM].reshape(KV_LORA_RANK, MLA_WIDTH)
    w_v = w_ukv_h[:, :, HEAD_DIM:].reshape(KV_LORA_RANK, MLA_WIDTH)
    return {
        "attn_g": attn_norm_g.reshape(1, D_MODEL),
        "w_main": w_main,
        "q_g": q_norm_g.reshape(1, Q_LORA_RANK),
        "kv_g": kv_norm_g.reshape(1, KV_LORA_RANK),
        "w_uq": _pad_heads(w_uq, MLA_QK_DIM).astype(BF16),
        "w_k": w_k.astype(BF16),
        "w_vt": w_v.T.astype(BF16),
        "w_navt": w_nav.T.astype(BF16),
        "rpb": na_rpb,
        "mla_out_g": mla_out_g.reshape(MLA_WIDTH, 1),
        "na_out_g": na_out_g.reshape(NA_WIDTH, 1),
        "w_o": w_o.astype(BF16),
        "ffn_g": ffn_norm_g.reshape(1, D_MODEL),
        "w_gate": w_gate.astype(BF16),
        "w_up": w_up.astype(BF16),
        "w_down": w_down.astype(BF16),
        "final_g": final_norm_g.reshape(1, D_MODEL),
    }


def _rope_tables(seq):
    half = MLA_ROPE_DIM // 2
    inv = ROPE_THETA ** (-jnp.arange(0, MLA_ROPE_DIM, 2, dtype=F32) / MLA_ROPE_DIM)
    ang = jnp.arange(seq, dtype=F32)[:, None] * inv[None, :]
    cos, sin = jnp.cos(ang), jnp.sin(ang)
    zeros = jnp.zeros((seq, half), F32)
    ones = jnp.ones((seq, HEAD_DIM), F32)
    tail = jnp.zeros((seq, V7X_LANES - MLA_QK_DIM), F32)
    cos_t = jnp.concatenate([ones, cos, cos, tail], axis=1)
    lead = jnp.zeros((seq, HEAD_DIM), F32)
    sin_lo = jnp.concatenate([lead, -sin, zeros, tail], axis=1)
    sin_hi = jnp.concatenate([lead, zeros, sin, tail], axis=1)
    return cos_t, sin_lo, sin_hi


def _na_bias_mask(rpb):
    qc = np.arange(GRID_W)[None, :]
    kc = np.arange(GRID_W)[:, None]
    col_start = np.clip(qc - NA_KW // 2, 0, GRID_W - NA_KW)
    col_ok = (kc >= col_start) & (kc < col_start + NA_KW)
    period = 2 * GRID_W
    n_d = 2 * NA_KH - 1
    rpb = rpb * LOG2_E
    row = jnp.concatenate([rpb[..., NA_KW - 1::-1],
                           jnp.zeros((HEADS, n_d, period - (2 * NA_KW - 1)), F32),
                           rpb[..., :NA_KW - 1:-1]], axis=-1)
    skew = jnp.broadcast_to(row[:, :, None, :], (HEADS, n_d, GRID_W, period))
    skew = skew.reshape(HEADS, n_d, GRID_W * period)[..., :GRID_W * (period - 1)]
    toe = skew.reshape(HEADS, n_d, GRID_W, period - 1)[..., :GRID_W]

    i = np.arange(NA_CHUNKS * NA_ROWS)[:, None]
    j = np.arange(NA_ROWS)[None, :]
    di = np.clip(i - NA_ROWS - j + NA_KH - 1, 0, n_d - 1)
    row_ok = np.stack([(i >= NA_ROWS) & (j >= 0),
                       (i - j >= 0) & (i - j < NA_KH),
                       (i < NA_KH) & (j >= 0)])
    ok = row_ok[:, :, None, :, None] & col_ok[None, None, :, None, :]
    bias = toe[:, di].transpose(0, 1, 3, 2, 4)
    bm = jnp.where(ok[:, None], bias[None], NEG_INF)
    assert row_ok.shape[0] == NA_KINDS
    return bm.reshape(NA_KINDS, HEADS, NA_CHUNKS, NA_TOK, NA_TOK)


def _run_trunk(x, w, bias_mask):
    batch, seq, _ = x.shape
    assert seq % TOKEN_TILE == 0 and seq % (2 * NA_TOK) == 0
    assert seq % min(seq, MLA_Q_BLOCK) == 0
    x2d = x.reshape(batch * seq, D_MODEL)
    cos_t, sin_lo, sin_hi = _rope_tables(seq)
    wt = dict(w, cos=cos_t, sin_lo=sin_lo, sin_hi=sin_hi)
    q, k, vt, naq, nak, navt = _proj_call(x2d, seq, wt)
    at = _mla_call(q, k, vt, batch, seq)
    bt = _na_call(naq, nak, navt, bias_mask, batch, seq)
    y = _post_call(at, bt, x2d, w)
    return y.reshape(batch, seq, D_MODEL)


def kernel(x_prompt, x_sample, attn_norm_g, w_in, q_norm_g, kv_norm_g, w_uq, w_ukv, na_rpb,
           mla_out_g, na_out_g, w_o, ffn_norm_g, w_gate, w_up, w_down, final_norm_g):
    assert attn_norm_g.shape[0] == 1, "single-layer trunk"
    w = _prep_layer(attn_norm_g[0], w_in[0], q_norm_g[0], kv_norm_g[0], w_uq[0], w_ukv[0], na_rpb[0],
                    mla_out_g[0], na_out_g[0], w_o[0], ffn_norm_g[0], w_gate[0], w_up[0], w_down[0],
                    final_norm_g)
    bias_mask = _na_bias_mask(w["rpb"])
    return (_run_trunk(x_prompt, w, bias_mask), _run_trunk(x_sample, w, bias_mask))
```

```python
import functools
import math

import jax
import jax.numpy as jnp
import numpy as np
from jax import lax
from jax.experimental import pallas as pl
from jax.experimental.pallas import tpu as pltpu

D_MODEL = 1024
MLA_WIDTH = D_MODEL // 2
NA_WIDTH = D_MODEL - MLA_WIDTH
HEADS = 8
HEAD_DIM = 64
MLA_ROPE_DIM = 32
MLA_QK_DIM = HEAD_DIM + MLA_ROPE_DIM
Q_LORA_RANK = 384
KV_LORA_RANK = 256
ROPE_THETA = 10000.0
GRID_W = 64
NA_KH = 8
NA_KW = 16
D_FF = int(math.ceil(8 * D_MODEL / 3 / 256)) * 256
EPS = 1e-6
NEG_INF = -1e30
LOG2_E = math.log2(math.e)

V7X_LANES = 128
V7X_BF16_SUBLANES = 16
V7X_VMEM_BYTES = 64 * 1024 * 1024
VMEM_SPILL_ALLOWANCE = 8 * 1024 * 1024

V_ROWS = HEAD_DIM + V7X_BF16_SUBLANES

TOKEN_TILE = 512
PROJ_TILE = 1024
MLA_Q_TILE = 256
MLA_Q_BLOCK = 4096
MLA_KV_GROUP = 4
MLA_MIN_SWEEPS = 8
NA_ROWS = 4
NA_TOK = NA_ROWS * GRID_W
NA_CHUNKS = 3
NA_GROUPS = 8
NA_KINDS = 3
FF_CHUNK = 256

_C_Q = 0
_C_KV = _C_Q + Q_LORA_RANK
_C_KR = _C_KV + KV_LORA_RANK
_C_NAQ = _C_KR + V7X_LANES
_C_NAK = _C_NAQ + NA_WIDTH
_C_END = _C_NAK + NA_WIDTH

_NT = (((1,), (1,)), ((), ()))
_TN = (((0,), (0,)), ((), ()))
BF16 = jnp.bfloat16
F32 = jnp.float32


def _rms_rows(x, g):
    return x * lax.rsqrt(jnp.mean(x * x, axis=-1, keepdims=True) + EPS) * g


def _rms_cols(xt, g):
    return xt * lax.rsqrt(jnp.mean(xt * xt, axis=0, keepdims=True) + EPS) * g


def _rope_lanes(x, cos, sin_lo, sin_hi):
    half = MLA_ROPE_DIM // 2
    return (x * cos + pltpu.roll(x, V7X_LANES - half, 1) * sin_lo
            + pltpu.roll(x, half, 1) * sin_hi)


def _rows(start, size):
    return pl.ds(start if isinstance(start, int) else pl.multiple_of(start, size), size)


def _launch(body, name, grid, operands, in_specs, out_specs, out_shape, scratch_shapes=()):
    def window(spec, dtype):
        buffers = 1 if spec.pipeline_mode is not None else 2
        return buffers * math.prod(spec.block_shape) * jnp.dtype(dtype).itemsize

    multi = isinstance(out_shape, (tuple, list))
    outs = tuple(out_shape) if multi else (out_shape,)
    o_specs = tuple(out_specs) if multi else (out_specs,)
    vmem = (sum(window(s, a.dtype) for s, a in zip(in_specs, operands))
            + sum(window(s, o.dtype) for s, o in zip(o_specs, outs))
            + sum(math.prod(s.shape) * jnp.dtype(s.dtype).itemsize for s in scratch_shapes)
            + VMEM_SPILL_ALLOWANCE)
    assert vmem <= V7X_VMEM_BYTES, (name, vmem)
    return pl.pallas_call(
        body, grid=grid, in_specs=list(in_specs), out_specs=out_specs, out_shape=out_shape,
        scratch_shapes=list(scratch_shapes),
        compiler_params=pltpu.CompilerParams(
            dimension_semantics=("arbitrary",) * len(grid), vmem_limit_bytes=vmem),
        name=name,
    )(*operands)


def _proj_kernel(x_ref, ga_ref, wmain_ref, gq_ref, gkv_ref, wuq_ref, wk_ref, wvt_ref, wnavt_ref,
                 cos_ref, sinlo_ref, sinhi_ref,
                 q_ref, k_ref, vt_ref, naq_ref, nak_ref, navt_ref):
    h = _rms_rows(x_ref[...], ga_ref[...]).astype(BF16)

    def proj(lo, hi):
        return jnp.dot(h, wmain_ref[:, lo:hi], preferred_element_type=F32)

    cqn = _rms_rows(proj(_C_Q, _C_KV), gq_ref[...]).astype(BF16)
    ckvn = _rms_rows(proj(_C_KV, _C_KR), gkv_ref[...]).astype(BF16)
    cos, sin_lo, sin_hi = cos_ref[...], sinlo_ref[...], sinhi_ref[...]
    k_rope = _rope_lanes(proj(_C_KR, _C_NAQ), cos, sin_lo, sin_hi)

    q = jnp.dot(cqn, wuq_ref[...], preferred_element_type=F32)
    kn = jnp.dot(ckvn, wk_ref[...], preferred_element_type=F32)
    vt = lax.dot_general(wvt_ref[...], ckvn, _NT, preferred_element_type=F32)
    naq = proj(_C_NAQ, _C_NAK) * (HEAD_DIM ** -0.5 * LOG2_E)
    scale = MLA_QK_DIM ** -0.5 * LOG2_E
    ones = jnp.ones((V_ROWS - HEAD_DIM, vt.shape[1]), BF16)
    lane_half = lax.broadcasted_iota(jnp.int32, (1, V7X_LANES), 1) // HEAD_DIM
    for hd in range(HEADS):
        lanes = slice(hd * V7X_LANES, (hd + 1) * V7X_LANES)
        q_ref[hd] = (_rope_lanes(q[:, lanes], cos, sin_lo, sin_hi) * scale).astype(BF16)
        pair = kn[:, (hd // 2) * V7X_LANES:(hd // 2 + 1) * V7X_LANES]
        nope = pair if hd % 2 == 0 else pltpu.roll(pair, HEAD_DIM, 1)
        k_ref[hd] = (jnp.where(lane_half == 0, nope, 0.0) + k_rope).astype(BF16)
        for c in range(vt.shape[1] // TOKEN_TILE):
            cols = slice(c * TOKEN_TILE, (c + 1) * TOKEN_TILE)
            vt_ref[hd, c, :HEAD_DIM, :] = vt[hd * HEAD_DIM:(hd + 1) * HEAD_DIM, cols].astype(BF16)
            vt_ref[hd, c, HEAD_DIM:, :] = ones[:, cols]
        na_pair = naq[:, (hd // 2) * V7X_LANES:(hd // 2 + 1) * V7X_LANES]
        naq_ref[hd] = jnp.where(lane_half == hd % 2, na_pair, 0.0).astype(BF16)
    nak = proj(_C_NAK, _C_END).astype(BF16)
    for pr in range(HEADS // 2):
        nak_ref[pr] = nak[:, pr * V7X_LANES:(pr + 1) * V7X_LANES]
    navt = lax.dot_general(wnavt_ref[...], h, _NT, preferred_element_type=F32).astype(BF16)
    for hd in range(HEADS):
        navt_ref[hd * V_ROWS:hd * V_ROWS + HEAD_DIM, :] = navt[hd * HEAD_DIM:(hd + 1) * HEAD_DIM, :]
        navt_ref[hd * V_ROWS + HEAD_DIM:(hd + 1) * V_ROWS, :] = ones


def _proj_call(x2d, seq, w):
    n = x2d.shape[0]
    tm = PROJ_TILE
    n_tiles = n // tm
    pos_tiles = seq // tm
    const = lambda i: (0, 0)
    single = dict(pipeline_mode=pl.Buffered(1))
    in_specs = [
        pl.BlockSpec((tm, D_MODEL), lambda i: (i, 0)),
        pl.BlockSpec((1, D_MODEL), const),
        pl.BlockSpec((D_MODEL, _C_END), const, **single),
        pl.BlockSpec((1, Q_LORA_RANK), const),
        pl.BlockSpec((1, KV_LORA_RANK), const),
        pl.BlockSpec((Q_LORA_RANK, HEADS * V7X_LANES), const, **single),
        pl.BlockSpec((KV_LORA_RANK, MLA_WIDTH), const, **single),
        pl.BlockSpec((MLA_WIDTH, KV_LORA_RANK), const, **single),
        pl.BlockSpec((NA_WIDTH, D_MODEL), const, **single),
        pl.BlockSpec((tm, V7X_LANES), lambda i: (i % pos_tiles, 0)),
        pl.BlockSpec((tm, V7X_LANES), lambda i: (i % pos_tiles, 0)),
        pl.BlockSpec((tm, V7X_LANES), lambda i: (i % pos_tiles, 0)),
    ]
    out_shape = (
        jax.ShapeDtypeStruct((HEADS, n, V7X_LANES), BF16),
        jax.ShapeDtypeStruct((HEADS, n, V7X_LANES), BF16),
        jax.ShapeDtypeStruct((HEADS, n // TOKEN_TILE, V_ROWS, TOKEN_TILE), BF16),
        jax.ShapeDtypeStruct((HEADS, n, V7X_LANES), BF16),
        jax.ShapeDtypeStruct((HEADS // 2, n, V7X_LANES), BF16),
        jax.ShapeDtypeStruct((HEADS * V_ROWS, n), BF16),
    )
    out_specs = (
        pl.BlockSpec((HEADS, tm, V7X_LANES), lambda i: (0, i, 0)),
        pl.BlockSpec((HEADS, tm, V7X_LANES), lambda i: (0, i, 0)),
        pl.BlockSpec((HEADS, tm // TOKEN_TILE, V_ROWS, TOKEN_TILE), lambda i: (0, i, 0, 0)),
        pl.BlockSpec((HEADS, tm, V7X_LANES), lambda i: (0, i, 0)),
        pl.BlockSpec((HEADS // 2, tm, V7X_LANES), lambda i: (0, i, 0)),
        pl.BlockSpec((HEADS * V_ROWS, tm), lambda i: (0, i)),
    )
    operands = (x2d, w["attn_g"], w["w_main"], w["q_g"], w["kv_g"], w["w_uq"], w["w_k"], w["w_vt"],
                w["w_navt"], w["cos"], w["sin_lo"], w["sin_hi"])
    return _launch(_proj_kernel, "proj", (n_tiles,), operands, in_specs, out_specs, out_shape)


def _mla_kernel(q_ref, k_ref, vt_ref, o_ref, acc_ref, s_ref, *, n_q, n_sc, n_u, group):
    tq, tk = MLA_Q_TILE, TOKEN_TILE
    acc_ref[...] = jnp.zeros_like(acc_ref)

    def split(u):
        return (u // n_sc, u % n_sc) if n_sc > 1 else (u, 0)

    def produce(u, qi):
        hd, sc = split(u)
        q = q_ref[hd, qi * tq:(qi + 1) * tq, :]
        col_max = []
        for c in range(group):
            k0 = pl.multiple_of((sc * group + c) * tk, tk)
            s = lax.dot_general(k_ref[hd, pl.ds(k0, tk), :], q, _NT, preferred_element_type=F32)
            s_ref[qi % 2, c] = s
            col_max.append(jnp.max(s, axis=0, keepdims=True))
        return functools.reduce(jnp.maximum, col_max)

    def consume(u, qi, m, cmax):
        hd, sc = split(u)
        m = jnp.where(sc == 0, NEG_INF, m)
        m_new = jnp.maximum(m, cmax)
        alpha = jnp.exp2(m - m_new)
        pv = 0.0
        for c in range(group):
            p = jnp.exp2(s_ref[qi % 2, c] - m_new).astype(BF16)
            pv = pv + jnp.dot(vt_ref[hd, sc * group + c], p, preferred_element_type=F32)
        acc = alpha * acc_ref[qi] + pv
        acc_ref[qi] = acc
        o_ref[_rows(hd * HEAD_DIM, HEAD_DIM), qi * tq:(qi + 1) * tq] = (
            acc[:HEAD_DIM] / acc[HEAD_DIM:HEAD_DIM + 1])
        return m_new

    def sweep(u, ms, cmax, last):
        new_ms = []
        for qi in range(n_q):
            if qi + 1 < n_q:
                cmax_next = produce(u, qi + 1)
            else:
                cmax_next = None if last else produce(u + 1, 0)
            new_ms.append(consume(u, qi, ms[qi], cmax))
            cmax = cmax_next
        return tuple(new_ms), cmax

    ms = tuple(jnp.full((1, tq), NEG_INF, F32) for _ in range(n_q))
    carry = lax.fori_loop(0, n_u - 1, lambda u, c: sweep(u, *c, False), (ms, produce(0, 0)))
    sweep(n_u - 1, *carry, True)


def _mla_call(q, k, vt, batch, seq):
    n = q.shape[1]
    qb = min(seq, MLA_Q_BLOCK)
    n_qb = seq // qb
    n_kv = seq // TOKEN_TILE
    n_q = qb // MLA_Q_TILE
    group = math.gcd(n_kv, MLA_KV_GROUP)
    n_sc = n_kv // group
    hb = max(1, min(HEADS, MLA_MIN_SWEEPS // n_sc))
    assert n_q % 2 == 0 and HEADS % hb == 0
    kernel = functools.partial(_mla_kernel, n_q=n_q, n_sc=n_sc, n_u=hb * n_sc, group=group)
    in_specs = [
        pl.BlockSpec((hb, qb, V7X_LANES), lambda b, h, i: (h, b * n_qb + i, 0)),
        pl.BlockSpec((hb, seq, V7X_LANES), lambda b, h, i: (h, b, 0)),
        pl.BlockSpec((hb, n_kv, V_ROWS, TOKEN_TILE), lambda b, h, i: (h, b, 0, 0)),
    ]
    return _launch(
        kernel, "mla", (batch, HEADS // hb, n_qb), (q, k, vt), in_specs,
        pl.BlockSpec((hb * HEAD_DIM, qb), lambda b, h, i: (h, b * n_qb + i)),
        jax.ShapeDtypeStruct((MLA_WIDTH, n), F32),
        scratch_shapes=[pltpu.VMEM((n_q, V_ROWS, MLA_Q_TILE), F32),
                        pltpu.VMEM((2, group, TOKEN_TILE, MLA_Q_TILE), F32)])


def _na_kernel(q_ref, *refs, groups, ng):
    n_win = ng + NA_CHUNKS - 1
    k_refs, v_refs = refs[:n_win], refs[n_win:2 * n_win]
    bm_ref, o_ref, s_ref = refs[2 * n_win:]
    first_group = ng * pl.program_id(1)
    kinds = [jnp.where(first_group + e == 0, 0, jnp.where(first_group + e == groups - 1, 2, 1))
             for e in range(ng)]

    def produce(hd, e):
        q = q_ref[hd, e * NA_TOK:(e + 1) * NA_TOK, :]
        col_max = []
        for t in range(NA_CHUNKS):
            s = (lax.dot_general(k_refs[e + t][hd // 2], q, _NT, preferred_element_type=F32)
                 + bm_ref[kinds[e], hd, t])
            s_ref[e % 2, t] = s
            col_max.append(jnp.max(s, axis=0, keepdims=True))
        return functools.reduce(jnp.maximum, col_max)

    def consume(hd, e, m):
        v_rows = _rows(hd * V_ROWS, V_ROWS)
        o = 0.0
        for t in range(NA_CHUNKS):
            p = jnp.exp2(s_ref[e % 2, t] - m).astype(BF16)
            o = o + jnp.dot(v_refs[e + t][v_rows, :], p, preferred_element_type=F32)
        o_ref[_rows(hd * HEAD_DIM, HEAD_DIM), e * NA_TOK:(e + 1) * NA_TOK] = (
            o[:HEAD_DIM] / o[HEAD_DIM:HEAD_DIM + 1])

    def head_pair(hd, m, last):
        items = [(h, e) for h in (hd, hd + 1) for e in range(ng)]
        for n, item in enumerate(items):
            if n + 1 < len(items):
                m_next = produce(*items[n + 1])
            else:
                m_next = None if last else produce(hd + 2, 0)
            consume(*item, m)
            m = m_next
        return m

    m = lax.fori_loop(0, HEADS // 2 - 1, lambda i, m: head_pair(2 * i, m, False), produce(0, 0))
    head_pair(HEADS - 2, m, True)


def _na_call(naq, nak, navt, bias_mask, batch, seq):
    n = nak.shape[1]
    groups = seq // NA_TOK
    ng = math.gcd(groups, NA_GROUPS)
    steps = groups // ng
    assert ng % 2 == 0

    def chunk(b, i, t):
        return b * groups + jnp.clip(ng * i - 1 + t, 0, groups - 1)

    n_win = ng + NA_CHUNKS - 1
    k_specs = [pl.BlockSpec((HEADS // 2, NA_TOK, V7X_LANES),
                            functools.partial(lambda b, i, t: (0, chunk(b, i, t), 0), t=t))
               for t in range(n_win)]
    v_specs = [pl.BlockSpec((HEADS * V_ROWS, NA_TOK),
                            functools.partial(lambda b, i, t: (0, chunk(b, i, t)), t=t))
               for t in range(n_win)]
    in_specs = ([pl.BlockSpec((HEADS, ng * NA_TOK, V7X_LANES),
                              lambda b, i: (0, b * steps + i, 0))]
                + k_specs + v_specs
                + [pl.BlockSpec((NA_KINDS, HEADS, NA_CHUNKS, NA_TOK, NA_TOK),
                                lambda b, i: (0, 0, 0, 0, 0),
                                pipeline_mode=pl.Buffered(1))])
    operands = (naq, *([nak] * n_win), *([navt] * n_win), bias_mask)
    return _launch(
        functools.partial(_na_kernel, groups=groups, ng=ng), "na", (batch, steps), operands, in_specs,
        pl.BlockSpec((NA_WIDTH, ng * NA_TOK), lambda b, i: (0, b * steps + i)),
        jax.ShapeDtypeStruct((NA_WIDTH, n), F32),
        scratch_shapes=[pltpu.VMEM((2, NA_CHUNKS, NA_TOK, NA_TOK), F32)])


def _post_kernel(at_ref, bt_ref, x_ref, ga_ref, gb_ref, wo_ref, gf_ref, wg_ref, wu_ref, wd_ref,
                 gl_ref, o_ref, acc_ref, gu_ref):
    mix_t = jnp.concatenate([_rms_cols(at_ref[...], ga_ref[...]),
                             _rms_cols(bt_ref[...], gb_ref[...])], axis=0).astype(BF16)
    x1 = x_ref[...] + lax.dot_general(mix_t, wo_ref[...], _TN, preferred_element_type=F32)
    h2 = _rms_rows(x1, gf_ref[...]).astype(BF16)
    acc_ref[...] = x1

    def produce(c, slot):
        cols = slice(c * FF_CHUNK, (c + 1) * FF_CHUNK)
        gu_ref[slot, 0] = jnp.dot(h2, wg_ref[:, cols], preferred_element_type=F32)
        gu_ref[slot, 1] = jnp.dot(h2, wu_ref[:, cols], preferred_element_type=F32)

    def consume(c, slot):
        gate = gu_ref[slot, 0]
        act = (gate * jax.nn.sigmoid(gate) * gu_ref[slot, 1]).astype(BF16)
        acc_ref[...] += jnp.dot(act, wd_ref[c * FF_CHUNK:(c + 1) * FF_CHUNK, :],
                                preferred_element_type=F32)

    n_ff = D_FF // FF_CHUNK
    produce(0, 0)
    for c in range(n_ff):
        if c + 1 < n_ff:
            produce(c + 1, (c + 1) % 2)
        consume(c, c % 2)
    o_ref[...] = _rms_rows(acc_ref[...], gl_ref[...])


def _post_call(at, bt, x2d, w):
    n = x2d.shape[0]
    tm = TOKEN_TILE
    const2 = lambda i: (0, 0)
    single = dict(pipeline_mode=pl.Buffered(1))
    in_specs = [
        pl.BlockSpec((MLA_WIDTH, tm), lambda i: (0, i)),
        pl.BlockSpec((NA_WIDTH, tm), lambda i: (0, i)),
        pl.BlockSpec((tm, D_MODEL), lambda i: (i, 0)),
        pl.BlockSpec((MLA_WIDTH, 1), const2),
        pl.BlockSpec((NA_WIDTH, 1), const2),
        pl.BlockSpec((D_MODEL, D_MODEL), const2, **single),
        pl.BlockSpec((1, D_MODEL), const2),
        pl.BlockSpec((D_MODEL, D_FF), const2, **single),
        pl.BlockSpec((D_MODEL, D_FF), const2, **single),
        pl.BlockSpec((D_FF, D_MODEL), const2, **single),
        pl.BlockSpec((1, D_MODEL), const2),
    ]
    operands = (at, bt, x2d, w["mla_out_g"], w["na_out_g"], w["w_o"], w["ffn_g"], w["w_gate"],
                w["w_up"], w["w_down"], w["final_g"])
    return _launch(
        _post_kernel, "post", (n // tm,), operands, in_specs,
        pl.BlockSpec((tm, D_MODEL), lambda i: (i, 0)), jax.ShapeDtypeStruct((n, D_MODEL), F32),
        scratch_shapes=[pltpu.VMEM((tm, D_MODEL), F32), pltpu.VMEM((2, 2, tm, FF_CHUNK), F32)])


def _pad_heads(wm, width):
    r = wm.shape[0]
    wm = wm.reshape(r, HEADS, width)
    return jnp.pad(wm, ((0, 0), (0, 0), (0, V7X_LANES - width))).reshape(r, HEADS * V7X_LANES)


def _prep_layer(attn_norm_g, w_in, q_norm_g, kv_norm_g, w_uq, w_ukv, na_rpb, mla_out_g, na_out_g,
                w_o, ffn_norm_g, w_gate, w_up, w_down, final_norm_g):
    o_kr = Q_LORA_RANK + KV_LORA_RANK
    o_na = o_kr + MLA_ROPE_DIM
    w_kr = w_in[:, o_kr:o_na]
    kr_block = jnp.zeros((D_MODEL, V7X_LANES), F32).at[:, HEAD_DIM:HEAD_DIM + MLA_ROPE_DIM].set(w_kr)
    w_nav = w_in[:, o_na + 2 * NA_WIDTH:]
    w_main = jnp.concatenate([w_in[:, :o_kr], kr_block, w_in[:, o_na:o_na + 2 * NA_WIDTH]],
                             axis=1).astype(BF16)

    w_ukv_h = w_ukv.reshape(KV_LORA_RANK, HEADS, 2 * HEAD_DIM)
    w_k = w_ukv_h[:, :, :HEAD_DIM].reshape(KV_LORA_RANK, MLA_WIDTH)
    w_v = w_ukv_h[:, :, HEAD_DIM:].reshape(KV_LORA_RANK, MLA_WIDTH)
    return {
        "attn_g": attn_norm_g.reshape(1, D_MODEL),
        "w_main": w_main,
        "q_g": q_norm_g.reshape(1, Q_LORA_RANK),
        "kv_g": kv_norm_g.reshape(1, KV_LORA_RANK),
        "w_uq": _pad_heads(w_uq, MLA_QK_DIM).astype(BF16),
        "w_k": w_k.astype(BF16),
        "w_vt": w_v.T.astype(BF16),
        "w_navt": w_nav.T.astype(BF16),
        "rpb": na_rpb,
        "mla_out_g": mla_out_g.reshape(MLA_WIDTH, 1),
        "na_out_g": na_out_g.reshape(NA_WIDTH, 1),
        "w_o": w_o.astype(BF16),
        "ffn_g": ffn_norm_g.reshape(1, D_MODEL),
        "w_gate": w_gate.astype(BF16),
        "w_up": w_up.astype(BF16),
        "w_down": w_down.astype(BF16),
        "final_g": final_norm_g.reshape(1, D_MODEL),
    }


def _rope_tables(seq):
    half = MLA_ROPE_DIM // 2
    inv = ROPE_THETA ** (-jnp.arange(0, MLA_ROPE_DIM, 2, dtype=F32) / MLA_ROPE_DIM)
    ang = jnp.arange(seq, dtype=F32)[:, None] * inv[None, :]
    cos, sin = jnp.cos(ang), jnp.sin(ang)
    zeros = jnp.zeros((seq, half), F32)
    ones = jnp.ones((seq, HEAD_DIM), F32)
    tail = jnp.zeros((seq, V7X_LANES - MLA_QK_DIM), F32)
    cos_t = jnp.concatenate([ones, cos, cos, tail], axis=1)
    lead = jnp.zeros((seq, HEAD_DIM), F32)
    sin_lo = jnp.concatenate([lead, -sin, zeros, tail], axis=1)
    sin_hi = jnp.concatenate([lead, zeros, sin, tail], axis=1)
    return cos_t, sin_lo, sin_hi


def _na_bias_mask(rpb):
    qc = np.arange(GRID_W)[None, :]
    kc = np.arange(GRID_W)[:, None]
    col_start = np.clip(qc - NA_KW // 2, 0, GRID_W - NA_KW)
    col_ok = (kc >= col_start) & (kc < col_start + NA_KW)
    period = 2 * GRID_W
    n_d = 2 * NA_KH - 1
    rpb = rpb * LOG2_E
    row = jnp.concatenate([rpb[..., NA_KW - 1::-1],
                           jnp.zeros((HEADS, n_d, period - (2 * NA_KW - 1)), F32),
                           rpb[..., :NA_KW - 1:-1]], axis=-1)
    skew = jnp.broadcast_to(row[:, :, None, :], (HEADS, n_d, GRID_W, period))
    skew = skew.reshape(HEADS, n_d, GRID_W * period)[..., :GRID_W * (period - 1)]
    toe = skew.reshape(HEADS, n_d, GRID_W, period - 1)[..., :GRID_W]

    i = np.arange(NA_CHUNKS * NA_ROWS)[:, None]
    j = np.arange(NA_ROWS)[None, :]
    di = np.clip(i - NA_ROWS - j + NA_KH - 1, 0, n_d - 1)
    row_ok = np.stack([(i >= NA_ROWS) & (j >= 0),
                       (i - j >= 0) & (i - j < NA_KH),
                       (i < NA_KH) & (j >= 0)])
    ok = row_ok[:, :, None, :, None] & col_ok[None, None, :, None, :]
    bias = toe[:, di].transpose(0, 1, 3, 2, 4)
    bm = jnp.where(ok[:, None], bias[None], NEG_INF)
    assert row_ok.shape[0] == NA_KINDS
    return bm.reshape(NA_KINDS, HEADS, NA_CHUNKS, NA_TOK, NA_TOK)


def _run_trunk(x, w, bias_mask):
    batch, seq, _ = x.shape
    assert seq % TOKEN_TILE == 0 and seq % (2 * NA_TOK) == 0
    assert seq % min(seq, MLA_Q_BLOCK) == 0
    x2d = x.reshape(batch * seq, D_MODEL)
    cos_t, sin_lo, sin_hi = _rope_tables(seq)
    wt = dict(w, cos=cos_t, sin_lo=sin_lo, sin_hi=sin_hi)
    q, k, vt, naq, nak, navt = _proj_call(x2d, seq, wt)
    at = _mla_call(q, k, vt, batch, seq)
    bt = _na_call(naq, nak, navt, bias_mask, batch, seq)
    y = _post_call(at, bt, x2d, w)
    return y.reshape(batch, seq, D_MODEL)


def kernel(x_prompt, x_sample, attn_norm_g, w_in, q_norm_g, kv_norm_g, w_uq, w_ukv, na_rpb,
           mla_out_g, na_out_g, w_o, ffn_norm_g, w_gate, w_up, w_down, final_norm_g):
    assert attn_norm_g.shape[0] == 1, "single-layer trunk"
    w = _prep_layer(attn_norm_g[0], w_in[0], q_norm_g[0], kv_norm_g[0], w_uq[0], w_ukv[0], na_rpb[0],
                    mla_out_g[0], na_out_g[0], w_o[0], ffn_norm_g[0], w_gate[0], w_up[0], w_down[0],
                    final_norm_g)
    bias_mask = _na_bias_mask(w["rpb"])
    return (_run_trunk(x_prompt, w, bias_mask), _run_trunk(x_sample, w, bias_mask))
```
